```python
import math
import jax
import jax.numpy as jnp
from jax import lax
import numpy as np


D_MODEL = 2048
BATCH = 8
SEQ = 4096
DEPTH = 2

SSM_EXPAND = 2
SSM_D_INNER = SSM_EXPAND * D_MODEL
SSM_HEAD_DIM = 64
SSM_HEADS = SSM_D_INNER // SSM_HEAD_DIM
SSM_GROUPS = 8
SSM_STATE = 128
SSM_CONV = 4
SSM_CHUNK = 128
SSM_BC = SSM_GROUPS * SSM_STATE
SSM_CONV_CH = SSM_D_INNER + 2 * SSM_BC
ATT_HEADS = 16
ATT_QK_DIM = 64
ATT_V_DIM = 2 * ATT_QK_DIM
ATT_QK_WIDTH = ATT_HEADS * 2 * ATT_QK_DIM
ATT_WIDTH = ATT_HEADS * ATT_V_DIM
ROPE_THETA = 500000.0
ROT_DIM = ATT_QK_DIM // 4
Q_BLOCK = 128
FFN_HIDDEN = -(-8 * D_MODEL // (3 * 256)) * 256
PLE_DIM = 256
EPS = 1e-6
SUBLN_EPS = 1e-5

OFF_Z = SSM_D_INNER
OFF_XBC = OFF_Z + SSM_CONV_CH
OFF_DT = OFF_XBC + SSM_HEADS
OFF_Q = OFF_DT + ATT_QK_WIDTH
OFF_K = OFF_Q + ATT_QK_WIDTH
OFF_V = OFF_K + ATT_WIDTH
OFF_GS = OFF_V + D_MODEL
IN_WIDTH = OFF_GS + D_MODEL

kernel_name = 'hybrid_ssd_diffattn_gated_block'


def rmsnorm(x, w, eps=EPS):
    xf = x.astype(jnp.float32)
    y = xf * lax.rsqrt(jnp.mean(xf * xf, axis=-1, keepdims=True) + eps)
    return (y * w.astype(jnp.float32)).astype(x.dtype)


def causal_depthwise_conv(u, w, b):
    c = u.shape[-1]
    out = lax.conv_general_dilated(
        u, w[:, None, :].astype(u.dtype), window_strides=(1,), padding=[(SSM_CONV - 1, 0)],
        dimension_numbers=('NWC', 'WIO', 'NWC'), feature_group_count=c)
    return out + b.astype(u.dtype)


def ssd_chunked_scan(xdt, da, bm, cm):
    b, l, h, pdim = xdt.shape
    g, n = bm.shape[2], bm.shape[3]
    k = h // g
    nc = l // SSM_CHUNK

    def to_chunks(t):
        t = t.reshape((b, nc, SSM_CHUNK) + t.shape[2:])
        return jnp.moveaxis(t, 1, 0)

    xs = (to_chunks(xdt.reshape(b, l, g, k, pdim)), to_chunks(da.reshape(b, l, g, k)),
          to_chunks(bm), to_chunks(cm))
    causal = jnp.tril(jnp.ones((SSM_CHUNK, SSM_CHUNK), dtype=bool))

    def step(state, inp):
        x_c, a_c, b_c, c_c = inp
        a_cs = jnp.cumsum(jnp.moveaxis(a_c, 1, -1), axis=-1)
        seg = a_cs[..., :, None] - a_cs[..., None, :]
        decay = jnp.exp(jnp.where(causal, seg, -jnp.inf))
        cb = jnp.einsum('bign,bjgn->bgij', c_c, b_c)
        y_diag = jnp.einsum('bgkij,bjgkp->bigkp', cb[:, :, None] * decay, x_c)
        y_off = jnp.einsum('bign,bgkpn->bigkp', c_c, state) * jnp.moveaxis(jnp.exp(a_cs), -1, 1)[..., None]
        to_end = jnp.exp(a_cs[..., -1:] - a_cs)
        new_state = state * jnp.exp(a_cs[..., -1])[..., None, None] + jnp.einsum(
            'bjgn,bgkj,bjgkp->bgkpn', b_c, to_end, x_c)
        return new_state, y_diag + y_off

    state0 = jnp.zeros((b, g, k, pdim, n), jnp.float32)
    _, ys = lax.scan(step, state0, xs)
    return jnp.moveaxis(ys, 0, 1).reshape(b, l, h, pdim)


def ssd_branch(z, xbc, dt_raw, conv_w, conv_b, dt_bias, a_log, d_skip, norm_w):
    b, l, _ = z.shape
    f32 = jnp.float32
    xbc = jax.nn.silu(causal_depthwise_conv(xbc, conv_w, conv_b))
    xs, bm, cm = jnp.split(xbc, [SSM_D_INNER, SSM_D_INNER + SSM_BC], axis=-1)
    xh = xs.reshape(b, l, SSM_HEADS, SSM_HEAD_DIM).astype(f32)
    dt = jax.nn.softplus(dt_raw.astype(f32) + dt_bias.astype(f32))
    a = -jnp.exp(a_log.astype(f32))
    y = ssd_chunked_scan(xh * dt[..., None], dt * a,
                         bm.reshape(b, l, SSM_GROUPS, SSM_STATE).astype(f32),
                         cm.reshape(b, l, SSM_GROUPS, SSM_STATE).astype(f32))
    y = y + d_skip.astype(f32)[:, None] * xh
    y = y.reshape(b, l, SSM_D_INNER) * jax.nn.silu(z.astype(f32))
    yg = y.reshape(b, l, SSM_GROUPS, SSM_D_INNER // SSM_GROUPS)
    yg = yg * lax.rsqrt(jnp.mean(yg * yg, axis=-1, keepdims=True) + EPS)
    y = yg.reshape(b, l, SSM_D_INNER) * norm_w.astype(f32)
    return y.astype(z.dtype)


def rope_partial(t, cos, sin):
    half = ROT_DIM // 2
    t1 = t[..., :half]
    t2 = t[..., half:ROT_DIM]
    rest = t[..., ROT_DIM:]
    out = jnp.concatenate([t1 * cos - t2 * sin, t2 * cos + t1 * sin, rest.astype(cos.dtype)], axis=-1)
    return out.astype(t.dtype)


def diff_attention(q, k, v, positions, q_norm_w, k_norm_w, lam_q1, lam_k1, lam_q2, lam_k2,
                   subln_w, lambda_init):
    b, l, _ = q.shape
    f32 = jnp.float32
    q = rmsnorm(q.reshape(b, l, ATT_HEADS, 2, ATT_QK_DIM), q_norm_w)
    k = rmsnorm(k.reshape(b, l, ATT_HEADS, 2, ATT_QK_DIM), k_norm_w)
    v = v.reshape(b, l, ATT_HEADS, ATT_V_DIM)
    inv_freq = ROPE_THETA ** (-jnp.arange(0, ROT_DIM, 2, dtype=f32) / ROT_DIM)
    ang = positions.astype(f32)[..., None] * inv_freq
    cos = jnp.cos(ang)[:, :, None, None, :]
    sin = jnp.sin(ang)[:, :, None, None, :]
    q = rope_partial(q, cos, sin)
    k = rope_partial(k, cos, sin)
    lam = (jnp.exp(jnp.sum(lam_q1.astype(f32) * lam_k1.astype(f32)))
           - jnp.exp(jnp.sum(lam_q2.astype(f32) * lam_k2.astype(f32))) + lambda_init)
    scale = ATT_QK_DIM ** -0.5
    nq = l // Q_BLOCK
    q_blocks = jnp.moveaxis(q.reshape(b, nq, Q_BLOCK, ATT_HEADS, 2, ATT_QK_DIM), 1, 0)
    starts = jnp.arange(nq, dtype=jnp.int32) * Q_BLOCK
    k_idx = jnp.arange(l, dtype=jnp.int32)

    def block(args):
        qb, start = args
        s = jnp.einsum('bqhcd,bkhcd->bhcqk', qb, k).astype(f32) * scale
        q_idx = start + jnp.arange(Q_BLOCK, dtype=jnp.int32)
        s = jnp.where(k_idx[None, :] <= q_idx[:, None], s, -jnp.inf)
        pr = jax.nn.softmax(s, axis=-1)
        attn = pr[:, :, 0] - lam * pr[:, :, 1]
        return jnp.einsum('bhqk,bkhd->bqhd', attn.astype(v.dtype), v)

    o = lax.map(block, (q_blocks, starts))
    o = jnp.moveaxis(o, 0, 1).reshape(b, l, ATT_HEADS, ATT_V_DIM)
    o = rmsnorm(o, subln_w, SUBLN_EPS) * (1.0 - lambda_init)
    return o.reshape(b, l, ATT_WIDTH)


def swiglu(h, w_gate_up, w_down):
    g, u = jnp.split(h @ w_gate_up, 2, axis=-1)
    return (jax.nn.silu(g) * u) @ w_down


def setup_inputs(seed: int = 0) -> dict:
    key = jax.random.key(seed)
    ks = jax.random.split(key, 28)
    f32 = jnp.float32

    def nrm(k, shape, scale):
        return jax.random.normal(k, shape, f32) * scale

    def gain(k, shape):
        return 1.0 + 0.02 * jax.random.normal(k, shape, f32)

    x = nrm(ks[0], (BATCH, SEQ, D_MODEL), 1.0)
    p = nrm(ks[1], (DEPTH, BATCH, SEQ, PLE_DIM), 1.0)
    positions = (jax.random.randint(ks[2], (BATCH, 1), 0, 1024, dtype=jnp.int32)
                 + jnp.arange(SEQ, dtype=jnp.int32)[None, :])
    w_in = nrm(ks[3], (DEPTH, D_MODEL, IN_WIDTH), D_MODEL ** -0.5)
    mix_norm_w = gain(ks[4], (DEPTH, D_MODEL))
    conv_w = nrm(ks[5], (DEPTH, SSM_CONV, SSM_CONV_CH), SSM_CONV ** -0.5)
    conv_b = nrm(ks[6], (DEPTH, SSM_CONV_CH), 0.01)
    u = jax.random.uniform(ks[7], (DEPTH, SSM_HEADS), f32)
    dt0 = jnp.maximum(jnp.exp(u * (math.log(0.1) - math.log(0.001)) + math.log(0.001)), 1e-4)
    dt_bias = dt0 + jnp.log(-jnp.expm1(-dt0))
    a_log = jnp.log(jax.random.uniform(ks[8], (DEPTH, SSM_HEADS), f32, 1.0, 16.0))
    d_skip = 1.0 + 0.1 * jax.random.normal(ks[9], (DEPTH, SSM_HEADS), f32)
    ssm_norm_w = gain(ks[10], (DEPTH, SSM_D_INNER))
    q_norm_w = gain(ks[11], (DEPTH, ATT_QK_DIM))
    k_norm_w = gain(ks[12], (DEPTH, ATT_QK_DIM))
    lambda_q1 = nrm(ks[13], (DEPTH, ATT_QK_DIM), 0.1)
    lambda_k1 = nrm(ks[14], (DEPTH, ATT_QK_DIM), 0.1)
    lambda_q2 = nrm(ks[15], (DEPTH, ATT_QK_DIM), 0.1)
    lambda_k2 = nrm(ks[16], (DEPTH, ATT_QK_DIM), 0.1)
    subln_w = gain(ks[17], (DEPTH, ATT_V_DIM))
    w_br_ssm = nrm(ks[18], (DEPTH, SSM_D_INNER, D_MODEL), SSM_D_INNER ** -0.5)
    w_br_attn = nrm(ks[19], (DEPTH, ATT_WIDTH, D_MODEL), ATT_WIDTH ** -0.5)
    w_out = nrm(ks[20], (DEPTH, D_MODEL, D_MODEL), D_MODEL ** -0.5)
    ffn_norm_w = gain(ks[21], (DEPTH, D_MODEL))
    w_gate_up = nrm(ks[22], (DEPTH, D_MODEL, 2 * FFN_HIDDEN), D_MODEL ** -0.5)
    w_down = nrm(ks[23], (DEPTH, FFN_HIDDEN, D_MODEL), FFN_HIDDEN ** -0.5)
    ple_norm_w = gain(ks[24], (DEPTH, D_MODEL))
    w_ple_gate = nrm(ks[25], (DEPTH, D_MODEL, D_MODEL), D_MODEL ** -0.5)
    w_ple = nrm(ks[26], (DEPTH, PLE_DIM, D_MODEL), PLE_DIM ** -0.5)
    return {'x': x, 'p': p, 'positions': positions, 'w_in': w_in, 'mix_norm_w': mix_norm_w,
            'conv_w': conv_w, 'conv_b': conv_b, 'dt_bias': dt_bias, 'a_log': a_log,
            'd_skip': d_skip, 'ssm_norm_w': ssm_norm_w, 'q_norm_w': q_norm_w,
            'k_norm_w': k_norm_w, 'lambda_q1': lambda_q1, 'lambda_k1': lambda_k1,
            'lambda_q2': lambda_q2, 'lambda_k2': lambda_k2, 'subln_w': subln_w,
            'w_br_ssm': w_br_ssm, 'w_br_attn': w_br_attn, 'w_out': w_out,
            'ffn_norm_w': ffn_norm_w, 'w_gate_up': w_gate_up, 'w_down': w_down,
            'ple_norm_w': ple_norm_w, 'w_ple_gate': w_ple_gate, 'w_ple': w_ple}


def reference(x, p, positions, w_in, mix_norm_w, conv_w, conv_b, dt_bias, a_log, d_skip,
              ssm_norm_w, q_norm_w, k_norm_w, lambda_q1, lambda_k1, lambda_q2, lambda_k2,
              subln_w, w_br_ssm, w_br_attn, w_out, ffn_norm_w, w_gate_up, w_down,
              ple_norm_w, w_ple_gate, w_ple):
    for i in range(DEPTH):
        lambda_init = 0.8 - 0.6 * math.exp(-0.3 * i)
        h = rmsnorm(x, mix_norm_w[i])
        z, xbc, dt_raw, q, k, v, g_ssm, g_att = jnp.split(
            h @ w_in[i], [OFF_Z, OFF_XBC, OFF_DT, OFF_Q, OFF_K, OFF_V, OFF_GS], axis=-1)
        y_ssm = ssd_branch(z, xbc, dt_raw, conv_w[i], conv_b[i], dt_bias[i], a_log[i],
                           d_skip[i], ssm_norm_w[i]) @ w_br_ssm[i]
        y_att = diff_attention(q, k, v, positions, q_norm_w[i], k_norm_w[i], lambda_q1[i],
                               lambda_k1[i], lambda_q2[i], lambda_k2[i], subln_w[i],
                               lambda_init) @ w_br_attn[i]
        mixed = jax.nn.sigmoid(g_ssm) * y_ssm + jax.nn.sigmoid(g_att) * y_att
        x = x + mixed @ w_out[i]
        x = x + swiglu(rmsnorm(x, ffn_norm_w[i]), w_gate_up[i], w_down[i])
        ple_gate = jax.nn.sigmoid(rmsnorm(x, ple_norm_w[i]) @ w_ple_gate[i])
        x = x + ple_gate * (p[i] @ w_ple[i])
    return x
```

```python
import functools
import math

import jax
import jax.numpy as jnp
from jax import lax
from jax.experimental import pallas as pl
from jax.experimental.pallas import tpu as pltpu

F32 = jnp.float32
BF16 = jnp.bfloat16

D_MODEL = 2048
SSM_D_INNER = 4096
SSM_HEAD_DIM = 64
SSM_HEADS = 64
SSM_GROUPS = 8
SSM_STATE = 128
SSM_CONV = 4
SSM_BC = SSM_GROUPS * SSM_STATE
GROUP_WIDTH = SSM_D_INNER // SSM_GROUPS
HEADS_PER_GROUP = SSM_HEADS // SSM_GROUPS
ATT_HEADS = 16
ATT_QK_DIM = 64
ATT_V_DIM = 128
ROPE_THETA = 500000.0
ROT_DIM = 16
FFN_HIDDEN = 5632
PLE_DIM = 256
EPS = 1e-6
SUBLN_EPS = 1e-5

OFF_Z = SSM_D_INNER
OFF_XBC = OFF_Z + SSM_D_INNER + 2 * SSM_BC
OFF_DT = OFF_XBC + SSM_HEADS
P_Z = 0
P_XS = P_Z + SSM_D_INNER
P_B = P_XS + SSM_D_INNER
P_C = P_B + SSM_BC
P_Q = P_C + SSM_BC
P_K = P_Q + ATT_HEADS * 2 * ATT_QK_DIM
P_V = P_K + ATT_HEADS * 2 * ATT_QK_DIM
P_GS = P_V + ATT_HEADS * ATT_V_DIM
P_GA = P_GS + D_MODEL
P_WIDTH = P_GA + D_MODEL

LANES = 128
SSD_CHUNK = 128
VMEM_LIMIT = 56 * 1024 * 1024


def _tile(n, want):
    t = min(n, want)
    while n % t:
        t -= 1
    return t


def _params(sem):
    return pltpu.CompilerParams(dimension_semantics=sem, vmem_limit_bytes=VMEM_LIMIT)


def _sigmoid(v):
    return 1.0 / (1.0 + jnp.exp(-v))


def _dot(a, b):
    return jnp.dot(a, b, preferred_element_type=F32)


def _split_bf16(v):
    hi = v.astype(BF16)
    lo = (v - hi.astype(F32)).astype(BF16)
    return hi, lo


def _in_proj_kernel(x_ref, nw_ref, w_ref, wdt_ref, o_ref, dt_ref, h_scr):
    @pl.when(pl.program_id(1) == 0)
    def _():
        x = x_ref[...]
        ms = jnp.mean(x * x, axis=-1, keepdims=True)
        h = (x * lax.rsqrt(ms + EPS) * nw_ref[...]).astype(BF16)
        h_scr[...] = h
        dt_ref[...] = _dot(h, wdt_ref[...])

    o_ref[...] = _dot(h_scr[...], w_ref[...]).astype(BF16)


def _in_proj(x2, norm_w, w_main, w_dt):
    t = x2.shape[0]
    tm = _tile(t, 1024)
    tn = 1024
    return pl.pallas_call(
        _in_proj_kernel,
        grid=(t // tm, P_WIDTH // tn),
        in_specs=[
            pl.BlockSpec((tm, D_MODEL), lambda i, j: (i, 0)),
            pl.BlockSpec((1, D_MODEL), lambda i, j: (0, 0)),
            pl.BlockSpec((D_MODEL, tn), lambda i, j: (0, j)),
            pl.BlockSpec((D_MODEL, LANES), lambda i, j: (0, 0)),
        ],
        out_specs=[
            pl.BlockSpec((tm, tn), lambda i, j: (i, j)),
            pl.BlockSpec((tm, LANES), lambda i, j: (i, 0)),
        ],
        out_shape=[
            jax.ShapeDtypeStruct((t, P_WIDTH), BF16),
            jax.ShapeDtypeStruct((t, LANES), F32),
        ],
        scratch_shapes=[pltpu.VMEM((tm, D_MODEL), BF16)],
        compiler_params=_params(("parallel", "arbitrary")),
        name="in_proj",
    )(x2, norm_w, w_main, w_dt)


def _dt_kernel(raw_ref, bias_ref, a_ref, dtc_ref, acsc_ref, dtr_ref, acsr_ref, *, nchunk):
    row = lax.broadcasted_iota(jnp.int32, (SSD_CHUNK, LANES), 0)
    for c in range(nchunk):
        rows = slice(c * SSD_CHUNK, (c + 1) * SSD_CHUNK)
        v = raw_ref[rows, :] + bias_ref[...]
        dt = jnp.maximum(v, 0.0) + jnp.log(1.0 + jnp.exp(-jnp.abs(v)))
        acs = dt * a_ref[...]
        shift = 1
        while shift < SSD_CHUNK:
            acs = acs + jnp.where(row >= shift, pltpu.roll(acs, shift, 0), 0.0)
            shift *= 2
        dtc_ref[rows, :] = dt
        acsc_ref[rows, :] = acs
        dtr_ref[c] = dt.T
        acsr_ref[c] = acs.T


def _dt_prep(dt_raw, dt_bias, a_neg):
    t = dt_raw.shape[0]
    rb = _tile(t, 1024)
    nchunk = rb // SSD_CHUNK
    nct = t // SSD_CHUNK
    col = jax.ShapeDtypeStruct((t, LANES), F32)
    rowm = jax.ShapeDtypeStruct((nct, LANES, SSD_CHUNK), F32)
    return pl.pallas_call(
        functools.partial(_dt_kernel, nchunk=nchunk),
        grid=(t // rb,),
        in_specs=[
            pl.BlockSpec((rb, LANES), lambda i: (i, 0)),
            pl.BlockSpec((1, LANES), lambda i: (0, 0)),
            pl.BlockSpec((1, LANES), lambda i: (0, 0)),
        ],
        out_specs=[
            pl.BlockSpec((rb, LANES), lambda i: (i, 0)),
            pl.BlockSpec((rb, LANES), lambda i: (i, 0)),
            pl.BlockSpec((nchunk, LANES, SSD_CHUNK), lambda i: (i, 0, 0)),
            pl.BlockSpec((nchunk, LANES, SSD_CHUNK), lambda i: (i, 0, 0)),
        ],
        out_shape=[col, col, rowm, rowm],
        compiler_params=_params(("parallel",)),
        name="dt_prep",
    )(dt_raw, dt_bias, a_neg)


def _ssd_kernel(z_ref, xs_ref, b_ref, c_ref, acsc_ref, dtc_ref, acsr_ref, dtr_ref, e_ref,
                cwx_ref, cwb_ref, cwc_ref, cbx_ref, cbb_ref, cbc_ref, dsk_ref, nw_ref,
                y_ref, state, tailx, tailb, tailc):
    cs = SSD_CHUNK
    chunk = pl.program_id(1)
    g = pl.program_id(2)

    @pl.when(chunk == 0)
    def _():
        state[g] = jnp.zeros(state.shape[1:], F32)
        tailx[g] = jnp.zeros(tailx.shape[1:], F32)
        tailb[g] = jnp.zeros(tailb.shape[1:], F32)
        tailc[g] = jnp.zeros(tailc.shape[1:], F32)

    def conv_silu(u_ref, tail, w_ref, bias_ref):
        u = u_ref[...].astype(F32)
        t8 = tail[g]
        w = w_ref[...]
        width = u.shape[1]
        row8 = lax.broadcasted_iota(jnp.int32, (8, width), 0)
        acc = u * w[SSM_CONV - 1:SSM_CONV, :] + bias_ref[...]
        for s in range(1, SSM_CONV):
            r = pltpu.roll(u, s, 0)
            top = jnp.where(row8 < s, pltpu.roll(t8, s, 0), r[0:8, :])
            shifted = jnp.concatenate([top, r[8:, :]], axis=0)
            acc = acc + shifted * w[SSM_CONV - 1 - s:SSM_CONV - s, :]
        tail[g] = u[cs - 8:cs, :]
        return acc * _sigmoid(acc)

    xc = conv_silu(xs_ref, tailx, cwx_ref, cbx_ref)
    bc = conv_silu(b_ref, tailb, cwb_ref, cbb_ref)
    cc = conv_silu(c_ref, tailc, cwc_ref, cbc_ref)
    bt_b = bc.T.astype(BF16)
    cc_b = cc.astype(BF16)
    cb = _dot(cc_b, bt_b)

    acsc = acsc_ref[...]
    last = acsc[cs - 1:cs, :]
    e1 = jnp.exp(acsc)
    wte = dtc_ref[...] * jnp.exp(last - acsc)
    dec = jnp.broadcast_to(jnp.exp(last), (16, LANES))
    stack = jnp.concatenate([e1, wte, dec], axis=0)
    hi, lo = _split_bf16(stack)
    ex = _dot(jnp.concatenate([hi, lo], axis=0), e_ref[...])
    nst = 2 * cs + 16
    ex = ex[0:nst, :] + ex[nst:2 * nst, :]
    e1_e = ex[0:cs, :]
    wte_e = ex[cs:2 * cs, :]
    dec_e = ex[2 * cs:2 * cs + 1, :]

    acsr = acsr_ref[...]
    dtr = dtr_ref[...]
    ri = lax.broadcasted_iota(jnp.int32, (cs, cs), 0)
    ci = lax.broadcasted_iota(jnp.int32, (cs, cs), 1)
    causal = ri >= ci
    lane = lax.broadcasted_iota(jnp.int32, (cs, LANES), 1)
    pieces = []
    for m in range(HEADS_PER_GROUP // 2):
        mats = []
        for k in (2 * m, 2 * m + 1):
            acs_j = jnp.broadcast_to(acsr[k:k + 1, :], (cs, cs))
            seg = acs_j.T - acs_j
            decay = jnp.exp(jnp.where(causal, seg, -jnp.inf))
            dt_j = jnp.broadcast_to(dtr[k:k + 1, :], (cs, cs))
            mats.append((cb * decay * dt_j).astype(BF16))
        xp = xc[:, m * LANES:(m + 1) * LANES]
        x_lo = jnp.where(lane < SSM_HEAD_DIM, xp, 0.0).astype(BF16)
        x_hi = jnp.where(lane >= SSM_HEAD_DIM, xp, 0.0).astype(BF16)
        pieces.append(_dot(jnp.concatenate(mats, axis=1), jnp.concatenate([x_lo, x_hi], axis=0)))
    y_diag = jnp.concatenate(pieces, axis=1)

    st = state[g]
    y_off = _dot(cc_b, st.astype(BF16)) * e1_e
    y = y_diag + y_off + dsk_ref[...] * xc
    z = z_ref[...].astype(F32)
    y = y * (z * _sigmoid(z))
    ms = jnp.mean(y * y, axis=-1, keepdims=True)
    y_ref[...] = (y * lax.rsqrt(ms + EPS) * nw_ref[...]).astype(BF16)

    xw = (xc * wte_e).astype(BF16)
    state[g] = st * dec_e + _dot(bt_b, xw)


def _ssd(proj, acsc, dtc, acsr, dtr, expand, cw_x, cw_b, cw_c, cb_x, cb_b, cb_c, dskip, norm_w,
         batch, seq):
    t = proj.shape[0]
    cs = SSD_CHUNK
    nc = seq // cs
    gw = GROUP_WIDTH
    n = SSM_STATE

    def rowblk(b, c, g):
        return b * nc + c

    in_specs = [
        pl.BlockSpec((cs, gw), lambda b, c, g: (rowblk(b, c, g), P_Z // gw + g)),
        pl.BlockSpec((cs, gw), lambda b, c, g: (rowblk(b, c, g), P_XS // gw + g)),
        pl.BlockSpec((cs, n), lambda b, c, g: (rowblk(b, c, g), P_B // n + g)),
        pl.BlockSpec((cs, n), lambda b, c, g: (rowblk(b, c, g), P_C // n + g)),
        pl.BlockSpec((cs, LANES), lambda b, c, g: (rowblk(b, c, g), 0)),
        pl.BlockSpec((cs, LANES), lambda b, c, g: (rowblk(b, c, g), 0)),
        pl.BlockSpec((None, None, HEADS_PER_GROUP, cs), lambda b, c, g: (rowblk(b, c, g), g, 0, 0)),
        pl.BlockSpec((None, None, HEADS_PER_GROUP, cs), lambda b, c, g: (rowblk(b, c, g), g, 0, 0)),
        pl.BlockSpec((None, LANES, gw), lambda b, c, g: (g, 0, 0)),
        pl.BlockSpec((SSM_CONV, gw), lambda b, c, g: (0, g)),
        pl.BlockSpec((SSM_CONV, n), lambda b, c, g: (0, g)),
        pl.BlockSpec((SSM_CONV, n), lambda b, c, g: (0, g)),
        pl.BlockSpec((1, gw), lambda b, c, g: (0, g)),
        pl.BlockSpec((1, n), lambda b, c, g: (0, g)),
        pl.BlockSpec((1, n), lambda b, c, g: (0, g)),
        pl.BlockSpec((1, gw), lambda b, c, g: (0, g)),
        pl.BlockSpec((1, gw), lambda b, c, g: (0, g)),
    ]
    return pl.pallas_call(
        _ssd_kernel,
        grid=(batch, nc, SSM_GROUPS),
        in_specs=in_specs,
        out_specs=pl.BlockSpec((cs, gw), lambda b, c, g: (rowblk(b, c, g), g)),
        out_shape=jax.ShapeDtypeStruct((t, SSM_D_INNER), BF16),
        scratch_shapes=[
            pltpu.VMEM((SSM_GROUPS, n, gw), F32),
            pltpu.VMEM((SSM_GROUPS, 8, gw), F32),
            pltpu.VMEM((SSM_GROUPS, 8, n), F32),
            pltpu.VMEM((SSM_GROUPS, 8, n), F32),
        ],
        compiler_params=_params(("parallel", "arbitrary", "arbitrary")),
        name="ssd_scan",
    )(proj, proj, proj, proj, acsc, dtc, acsr, dtr, expand,
      cw_x, cw_b, cw_c, cb_x, cb_b, cb_c, dskip, norm_w)


def _attn_kernel(lam_ref, q_ref, k_ref, v_ref, cos_ref, s1_ref, s2_ref, qw_ref, kw_ref, sw_ref,
                 o_ref, kst, qp, vt, acc, m_scr, l_scr, *, blk, nblk, lambda_init):
    lane = lax.broadcasted_iota(jnp.int32, (blk, LANES), 1)
    ra = lax.broadcasted_iota(jnp.int32, (LANES, LANES), 0) // ATT_QK_DIM
    rb = lax.broadcasted_iota(jnp.int32, (LANES, LANES), 1) // ATT_QK_DIM
    half_mean = jnp.where(ra == rb, 1.0 / ATT_QK_DIM, 0.0).astype(BF16)

    def norm_rope(t, w, rows):
        hi, lo = _split_bf16(t * t)
        ms = _dot(hi, half_mean) + _dot(lo, half_mean)
        tn = t * lax.rsqrt(ms + EPS) * w
        return (tn * cos_ref[rows, :] + pltpu.roll(tn, ROT_DIM // 2, 1) * s1_ref[rows, :]
                + pltpu.roll(tn, LANES - ROT_DIM // 2, 1) * s2_ref[rows, :])

    def prep(r, carry):
        rows = pl.ds(pl.multiple_of(r * blk, blk), blk)
        kb = norm_rope(k_ref[rows, :].astype(F32), kw_ref[...], rows)
        kst[r, 0:blk, :] = jnp.where(lane < ATT_QK_DIM, kb, 0.0).astype(BF16)
        kst[r, blk:2 * blk, :] = jnp.where(lane >= ATT_QK_DIM, kb, 0.0).astype(BF16)
        qb = norm_rope(q_ref[rows, :].astype(F32), qw_ref[...], rows)
        qp[r] = (qb * (ATT_QK_DIM ** -0.5)).astype(BF16)
        vt[r] = v_ref[rows, :].astype(F32).T.astype(BF16)
        return carry

    lax.fori_loop(0, nblk, prep, 0)

    kidx = lax.broadcasted_iota(jnp.int32, (blk, blk), 0)
    qidx = lax.broadcasted_iota(jnp.int32, (blk, blk), 1)
    diag_mask = kidx <= qidx
    lam = lam_ref[0]

    def kv_step(qblk, kj, masked):
        s = lax.dot_general(kst[kj], qblk, (((1,), (1,)), ((), ())), preferred_element_type=F32)
        for c in range(2):
            sc = s[c * blk:(c + 1) * blk, :]
            if masked:
                sc = jnp.where(diag_mask, sc, -jnp.inf)
            m_old = m_scr[c]
            m_new = jnp.maximum(m_old, jnp.max(sc, axis=0, keepdims=True))
            alpha = jnp.exp(m_old - m_new)
            p = jnp.exp(sc - m_new)
            l_scr[c] = alpha * l_scr[c] + jnp.sum(p, axis=0, keepdims=True)
            acc[c] = alpha * acc[c] + _dot(vt[kj], p.astype(BF16))
            m_scr[c] = m_new

    def q_body(qi, carry):
        qblk = qp[qi]
        m_scr[...] = jnp.full(m_scr.shape, -jnp.inf, F32)
        l_scr[...] = jnp.zeros(l_scr.shape, F32)
        acc[...] = jnp.zeros(acc.shape, F32)

        def off_diag(kj, c2):
            kv_step(qblk, kj, False)
            return c2

        lax.fori_loop(0, qi, off_diag, 0)
        kv_step(qblk, qi, True)

        o = acc[0] / l_scr[0] - lam * (acc[1] / l_scr[1])
        ms = jnp.mean(o * o, axis=0, keepdims=True)
        on = o * lax.rsqrt(ms + SUBLN_EPS) * sw_ref[...] * (1.0 - lambda_init)
        rows = pl.ds(pl.multiple_of(qi * blk, blk), blk)
        o_ref[rows, :] = on.T.astype(BF16)
        return carry

    lax.fori_loop(0, nblk, q_body, 0)


def _attention(proj, lam, cos_t, s1_t, s2_t, qw, kw, sw, batch, seq, lambda_init):
    t = proj.shape[0]
    blk = _tile(seq, 512)
    nblk = seq // blk
    qoff, koff, voff = P_Q // LANES, P_K // LANES, P_V // LANES
    head_spec = lambda off: pl.BlockSpec((seq, LANES), lambda b, h: (b, off + h))
    tab_spec = pl.BlockSpec((seq, LANES), lambda b, h: (b, 0))
    return pl.pallas_call(
        functools.partial(_attn_kernel, blk=blk, nblk=nblk, lambda_init=lambda_init),
        grid=(batch, ATT_HEADS),
        in_specs=[
            pl.BlockSpec(memory_space=pltpu.SMEM),
            head_spec(qoff), head_spec(koff), head_spec(voff),
            tab_spec, tab_spec, tab_spec,
            pl.BlockSpec((1, LANES), lambda b, h: (0, 0)),
            pl.BlockSpec((1, LANES), lambda b, h: (0, 0)),
            pl.BlockSpec((ATT_V_DIM, blk), lambda b, h: (0, 0)),
        ],
        out_specs=pl.BlockSpec((seq, LANES), lambda b, h: (b, h)),
        out_shape=jax.ShapeDtypeStruct((t, ATT_HEADS * ATT_V_DIM), BF16),
        scratch_shapes=[
            pltpu.VMEM((nblk, 2 * blk, LANES), BF16),
            pltpu.VMEM((nblk, blk, LANES), BF16),
            pltpu.VMEM((nblk, ATT_V_DIM, blk), BF16),
            pltpu.VMEM((2, ATT_V_DIM, blk), F32),
            pltpu.VMEM((2, 1, blk), F32),
            pltpu.VMEM((2, 1, blk), F32),
        ],
        compiler_params=_params(("parallel", "arbitrary")),
        name="diff_attention",
    )(lam, proj, proj, proj, cos_t, s1_t, s2_t, qw, kw, sw)


def _merge_kernel(ys_ref, oa_ref, wbs_ref, wba_ref, gs_ref, ga_ref, o_ref):
    a = _dot(ys_ref[...], wbs_ref[...])
    b = _dot(oa_ref[...], wba_ref[...])
    gs = _sigmoid(gs_ref[...].astype(F32))
    ga = _sigmoid(ga_ref[...].astype(F32))
    o_ref[...] = (gs * a + ga * b).astype(BF16)


def _merge(y_ssm, o_att, w_bs, w_ba, proj):
    t = y_ssm.shape[0]
    tm = _tile(t, 512)
    tn = 1024
    return pl.pallas_call(
        _merge_kernel,
        grid=(t // tm, D_MODEL // tn),
        in_specs=[
            pl.BlockSpec((tm, SSM_D_INNER), lambda i, j: (i, 0)),
            pl.BlockSpec((tm, D_MODEL), lambda i, j: (i, 0)),
            pl.BlockSpec((SSM_D_INNER, tn), lambda i, j: (0, j)),
            pl.BlockSpec((D_MODEL, tn), lambda i, j: (0, j)),
            pl.BlockSpec((tm, tn), lambda i, j: (i, P_GS // tn + j)),
            pl.BlockSpec((tm, tn), lambda i, j: (i, P_GA // tn + j)),
        ],
        out_specs=pl.BlockSpec((tm, tn), lambda i, j: (i, j)),
        out_shape=jax.ShapeDtypeStruct((t, D_MODEL), BF16),
        compiler_params=_params(("parallel", "arbitrary")),
        name="branch_merge",
    )(y_ssm, o_att, w_bs, w_ba, proj, proj)


def _out_proj_kernel(mix_ref, wo_ref, x_ref, nw_ref, x1_ref, h_ref):
    x1 = x_ref[...] + _dot(mix_ref[...], wo_ref[...])
    x1_ref[...] = x1
    ms = jnp.mean(x1 * x1, axis=-1, keepdims=True)
    h_ref[...] = (x1 * lax.rsqrt(ms + EPS) * nw_ref[...]).astype(BF16)


def _out_proj(mixed, w_o, x2, norm_w):
    t = x2.shape[0]
    tm = _tile(t, 512)
    row = lambda i: (i, 0)
    const = lambda i: (0, 0)
    return pl.pallas_call(
        _out_proj_kernel,
        grid=(t // tm,),
        in_specs=[
            pl.BlockSpec((tm, D_MODEL), row),
            pl.BlockSpec((D_MODEL, D_MODEL), const),
            pl.BlockSpec((tm, D_MODEL), row),
            pl.BlockSpec((1, D_MODEL), const),
        ],
        out_specs=[pl.BlockSpec((tm, D_MODEL), row), pl.BlockSpec((tm, D_MODEL), row)],
        out_shape=[jax.ShapeDtypeStruct((t, D_MODEL), F32), jax.ShapeDtypeStruct((t, D_MODEL), BF16)],
        compiler_params=_params(("parallel",)),
        name="out_proj",
    )(mixed, w_o, x2, norm_w)


def _ffn_kernel(h_ref, wg_ref, wu_ref, wd_ref, x_ref, o_ref, acc):
    j = pl.program_id(1)
    h = h_ref[...]
    gate = _dot(h, wg_ref[...])
    up = _dot(h, wu_ref[...])
    part = _dot((gate * _sigmoid(gate) * up).astype(BF16), wd_ref[...])

    @pl.when(j == 0)
    def _():
        acc[...] = part

    @pl.when(j > 0)
    def _():
        acc[...] += part

    @pl.when(j == pl.num_programs(1) - 1)
    def _():
        o_ref[...] = x_ref[...] + acc[...]


def _ffn(h2, w_gu, w_d, x1):
    t = h2.shape[0]
    tm = _tile(t, 512)
    th = 512
    nh = FFN_HIDDEN // th
    return pl.pallas_call(
        _ffn_kernel,
        grid=(t // tm, nh),
        in_specs=[
            pl.BlockSpec((tm, D_MODEL), lambda i, j: (i, 0)),
            pl.BlockSpec((D_MODEL, th), lambda i, j: (0, j)),
            pl.BlockSpec((D_MODEL, th), lambda i, j: (0, nh + j)),
            pl.BlockSpec((th, D_MODEL), lambda i, j: (j, 0)),
            pl.BlockSpec((tm, D_MODEL), lambda i, j: (i, 0)),
        ],
        out_specs=pl.BlockSpec((tm, D_MODEL), lambda i, j: (i, 0)),
        out_shape=jax.ShapeDtypeStruct((t, D_MODEL), F32),
        scratch_shapes=[pltpu.VMEM((tm, D_MODEL), F32)],
        compiler_params=_params(("parallel", "arbitrary")),
        name="ffn",
    )(h2, w_gu, w_gu, w_d, x1)


def _ple_kernel(x_ref, nw_ref, wg_ref, p_ref, wp_ref, o_ref):
    x = x_ref[...]
    ms = jnp.mean(x * x, axis=-1, keepdims=True)
    h = (x * lax.rsqrt(ms + EPS) * nw_ref[...]).astype(BF16)
    gate = _sigmoid(_dot(h, wg_ref[...]))
    o_ref[...] = x + gate * _dot(p_ref[...].astype(BF16), wp_ref[...])


def _ple(x2, norm_w, w_gate, p2, w_ple):
    t = x2.shape[0]
    tm = _tile(t, 512)
    row = lambda i: (i, 0)
    const = lambda i: (0, 0)
    return pl.pallas_call(
        _ple_kernel,
        grid=(t // tm,),
        in_specs=[
            pl.BlockSpec((tm, D_MODEL), row),
            pl.BlockSpec((1, D_MODEL), const),
            pl.BlockSpec((D_MODEL, D_MODEL), const),
            pl.BlockSpec((tm, PLE_DIM), row),
            pl.BlockSpec((PLE_DIM, D_MODEL), const),
        ],
        out_specs=pl.BlockSpec((tm, D_MODEL), row),
        out_shape=jax.ShapeDtypeStruct((t, D_MODEL), F32),
        compiler_params=_params(("parallel",)),
        name="ple",
    )(x2, norm_w, w_gate, p2, w_ple)


def _pad_lanes(v, fill=0.0):
    out = jnp.full((1, LANES), fill, F32)
    return out.at[0, :v.shape[0]].set(v.astype(F32))


def _rope_tables(positions):
    half = ROT_DIM // 2
    inv_freq = ROPE_THETA ** (-jnp.arange(0, ROT_DIM, 2, dtype=F32) / ROT_DIM)
    ang = positions.astype(F32)[..., None] * inv_freq
    cos, sin = jnp.cos(ang), jnp.sin(ang)
    b, l = positions.shape
    rest = ATT_QK_DIM - ROT_DIM
    ones = jnp.ones((b, l, rest), F32)
    zeros = jnp.zeros((b, l, rest), F32)
    z8 = jnp.zeros((b, l, half), F32)
    cos_m = jnp.concatenate([cos, cos, ones], axis=-1)
    s1_m = jnp.concatenate([z8, sin, zeros], axis=-1)
    s2_m = jnp.concatenate([-sin, z8, zeros], axis=-1)
    flat = lambda m: jnp.concatenate([m, m], axis=-1).reshape(b * l, LANES)
    return flat(cos_m), flat(s1_m), flat(s2_m)


def kernel(x, p, positions, w_in, mix_norm_w, conv_w, conv_b, dt_bias, a_log, d_skip, ssm_norm_w, q_norm_w, k_norm_w, lambda_q1, lambda_k1, lambda_q2, lambda_k2, subln_w, w_br_ssm, w_br_attn, w_out, ffn_norm_w, w_gate_up, w_down, ple_norm_w, w_ple_gate, w_ple):
    depth = w_in.shape[0]
    batch, seq, _ = x.shape
    t = batch * seq
    nct = t // SSD_CHUNK
    blk = _tile(seq, 512)

    cos_t, s1_t, s2_t = _rope_tables(positions)
    head_of_row = jnp.arange(LANES)[:, None]
    head_of_col = jnp.arange(GROUP_WIDTH)[None, :] // SSM_HEAD_DIM
    expand = jnp.stack([(head_of_row == g * HEADS_PER_GROUP + head_of_col) for g in range(SSM_GROUPS)]
                       ).astype(BF16)

    xr = x.reshape(t, D_MODEL)
    for i in range(depth):
        lambda_init = 0.8 - 0.6 * math.exp(-0.3 * i)
        wi = w_in[i]
        w_main = jnp.concatenate([wi[:, :OFF_XBC], wi[:, OFF_DT:]], axis=1).astype(BF16)
        w_dt = jnp.pad(wi[:, OFF_XBC:OFF_DT], ((0, 0), (0, LANES - SSM_HEADS))).astype(BF16)

        proj, dt_raw = _in_proj(xr, mix_norm_w[i][None, :], w_main, w_dt)

        a_neg = _pad_lanes(-jnp.exp(a_log[i].astype(F32)))
        dtc, acsc, dtr, acsr = _dt_prep(dt_raw, _pad_lanes(dt_bias[i]), a_neg)
        grp = lambda m: m[:, :SSM_HEADS, :].reshape(nct, SSM_GROUPS, HEADS_PER_GROUP, SSD_CHUNK)
        cw = conv_w[i].astype(F32)
        cbias = conv_b[i].astype(F32)[None, :]
        y_ssm = _ssd(
            proj, acsc, dtc, grp(acsr), grp(dtr), expand,
            cw[:, :SSM_D_INNER], cw[:, SSM_D_INNER:SSM_D_INNER + SSM_BC], cw[:, SSM_D_INNER + SSM_BC:],
            cbias[:, :SSM_D_INNER], cbias[:, SSM_D_INNER:SSM_D_INNER + SSM_BC], cbias[:, SSM_D_INNER + SSM_BC:],
            jnp.repeat(d_skip[i].astype(F32), SSM_HEAD_DIM)[None, :], ssm_norm_w[i].astype(F32)[None, :],
            batch, seq)

        lam = (jnp.exp(jnp.sum(lambda_q1[i].astype(F32) * lambda_k1[i].astype(F32)))
               - jnp.exp(jnp.sum(lambda_q2[i].astype(F32) * lambda_k2[i].astype(F32))) + lambda_init)
        two = lambda v: jnp.concatenate([v, v]).astype(F32)[None, :]
        sw = jnp.broadcast_to(subln_w[i].astype(F32)[:, None], (ATT_V_DIM, blk))
        o_att = _attention(proj, lam.reshape(1), cos_t, s1_t, s2_t, two(q_norm_w[i]), two(k_norm_w[i]),
                           sw, batch, seq, lambda_init)

        mixed = _merge(y_ssm, o_att, w_br_ssm[i].astype(BF16), w_br_attn[i].astype(BF16), proj)
        x1, h2 = _out_proj(mixed, w_out[i].astype(BF16), xr, ffn_norm_w[i][None, :])
        x2 = _ffn(h2, w_gate_up[i].astype(BF16), w_down[i].astype(BF16), x1)
        xr = _ple(x2, ple_norm_w[i][None, :], w_ple_gate[i].astype(BF16),
                  p[i].reshape(t, PLE_DIM), w_ple[i].astype(BF16))
    return xr.reshape(batch, seq, D_MODEL)
```

```python
import functools
import math

import jax
import jax.numpy as jnp
from jax import lax
from jax.experimental import pallas as pl
from jax.experimental.pallas import tpu as pltpu

F32 = jnp.float32
BF16 = jnp.bfloat16

D_MODEL = 2048
SSM_D_INNER = 4096
SSM_HEAD_DIM = 64
SSM_HEADS = 64
SSM_GROUPS = 8
SSM_STATE = 128
SSM_CONV = 4
SSM_BC = SSM_GROUPS * SSM_STATE
GROUP_WIDTH = SSM_D_INNER // SSM_GROUPS
HEADS_PER_GROUP = SSM_HEADS // SSM_GROUPS
ATT_HEADS = 16
ATT_QK_DIM = 64
ATT_V_DIM = 128
ROPE_THETA = 500000.0
ROT_DIM = 16
FFN_HIDDEN = 5632
PLE_DIM = 256
EPS = 1e-6
SUBLN_EPS = 1e-5

OFF_Z = SSM_D_INNER
OFF_XBC = OFF_Z + SSM_D_INNER + 2 * SSM_BC
OFF_DT = OFF_XBC + SSM_HEADS
P_Z = 0
P_XS = P_Z + SSM_D_INNER
P_B = P_XS + SSM_D_INNER
P_C = P_B + SSM_BC
P_Q = P_C + SSM_BC
P_K = P_Q + ATT_HEADS * 2 * ATT_QK_DIM
P_V = P_K + ATT_HEADS * 2 * ATT_QK_DIM
P_GS = P_V + ATT_HEADS * ATT_V_DIM
P_GA = P_GS + D_MODEL
P_WIDTH = P_GA + D_MODEL

LANES = 128
SSD_CHUNK = 128
SSD_GROUPS_PER_STEP = 8
VMEM_LIMIT = 56 * 1024 * 1024


def _tile(n, want):
    t = min(n, want)
    while n % t:
        t -= 1
    return t


def _params(sem):
    return pltpu.CompilerParams(dimension_semantics=sem, vmem_limit_bytes=VMEM_LIMIT)


def _sigmoid(v):
    return 1.0 / (1.0 + jnp.exp2(v * (-math.log2(math.e))))


def _dot(a, b):
    return jnp.dot(a, b, preferred_element_type=F32)


def _split_bf16(v):
    hi = v.astype(BF16)
    lo = (v - hi.astype(F32)).astype(BF16)
    return hi, lo


def _in_proj_kernel(x_ref, nw_ref, w_ref, wdt_ref, o_ref, dt_ref, h_scr):
    @pl.when(pl.program_id(1) == 0)
    def _():
        x = x_ref[...]
        ms = jnp.mean(x * x, axis=-1, keepdims=True)
        h = (x * lax.rsqrt(ms + EPS) * nw_ref[...]).astype(BF16)
        h_scr[...] = h
        dt_ref[...] = _dot(h, wdt_ref[...])

    o_ref[...] = _dot(h_scr[...], w_ref[...]).astype(BF16)


def _in_proj(x2, norm_w, w_main, w_dt):
    t = x2.shape[0]
    tm = _tile(t, 1024)
    tn = 1024
    return pl.pallas_call(
        _in_proj_kernel,
        grid=(t // tm, P_WIDTH // tn),
        in_specs=[
            pl.BlockSpec((tm, D_MODEL), lambda i, j: (i, 0)),
            pl.BlockSpec((1, D_MODEL), lambda i, j: (0, 0)),
            pl.BlockSpec((D_MODEL, tn), lambda i, j: (0, j)),
            pl.BlockSpec((D_MODEL, LANES), lambda i, j: (0, 0)),
        ],
        out_specs=[
            pl.BlockSpec((tm, tn), lambda i, j: (i, j)),
            pl.BlockSpec((tm, LANES), lambda i, j: (i, 0)),
        ],
        out_shape=[
            jax.ShapeDtypeStruct((t, P_WIDTH), BF16),
            jax.ShapeDtypeStruct((t, LANES), F32),
        ],
        scratch_shapes=[pltpu.VMEM((tm, D_MODEL), BF16)],
        compiler_params=_params(("parallel", "arbitrary")),
        name="in_proj",
    )(x2, norm_w, w_main, w_dt)


def _dt_kernel(raw_ref, bias_ref, a_ref, dtc_ref, acsc_ref, dtr_ref, acsr_ref, *, nchunk):
    row = lax.broadcasted_iota(jnp.int32, (SSD_CHUNK, LANES), 0)
    for c in range(nchunk):
        rows = slice(c * SSD_CHUNK, (c + 1) * SSD_CHUNK)
        v = raw_ref[rows, :] + bias_ref[...]
        dt = jnp.maximum(v, 0.0) + jnp.log(1.0 + jnp.exp(-jnp.abs(v)))
        acs = dt * a_ref[...]
        shift = 1
        while shift < SSD_CHUNK:
            acs = acs + jnp.where(row >= shift, pltpu.roll(acs, shift, 0), 0.0)
            shift *= 2
        dtc_ref[rows, :] = dt
        acsc_ref[rows, :] = acs
        dtr_ref[c] = dt.T
        acsr_ref[c] = acs.T


def _dt_prep(dt_raw, dt_bias, a_neg):
    t = dt_raw.shape[0]
    rb = _tile(t, 1024)
    nchunk = rb // SSD_CHUNK
    nct = t // SSD_CHUNK
    col = jax.ShapeDtypeStruct((t, LANES), F32)
    rowm = jax.ShapeDtypeStruct((nct, LANES, SSD_CHUNK), F32)
    return pl.pallas_call(
        functools.partial(_dt_kernel, nchunk=nchunk),
        grid=(t // rb,),
        in_specs=[
            pl.BlockSpec((rb, LANES), lambda i: (i, 0)),
            pl.BlockSpec((1, LANES), lambda i: (0, 0)),
            pl.BlockSpec((1, LANES), lambda i: (0, 0)),
        ],
        out_specs=[
            pl.BlockSpec((rb, LANES), lambda i: (i, 0)),
            pl.BlockSpec((rb, LANES), lambda i: (i, 0)),
            pl.BlockSpec((nchunk, LANES, SSD_CHUNK), lambda i: (i, 0, 0)),
            pl.BlockSpec((nchunk, LANES, SSD_CHUNK), lambda i: (i, 0, 0)),
        ],
        out_shape=[col, col, rowm, rowm],
        compiler_params=_params(("parallel",)),
        name="dt_prep",
    )(dt_raw, dt_bias, a_neg)


def _ssd_kernel(z_ref, xs_ref, b_ref, c_ref, acsc_ref, dtc_ref, acsr_ref, dtr_ref, e_ref,
                cwx_ref, cwb_ref, cwc_ref, cbx_ref, cbb_ref, cbc_ref, dsk_ref, nw_ref,
                y_ref, state, winx, winb, winc, *, gps):
    cs = SSD_CHUNK
    gw = GROUP_WIDTH
    n = SSM_STATE
    chunk = pl.program_id(1)
    g0 = pl.program_id(2) * gps

    @pl.when(chunk == 0)
    def _():
        for gi in range(gps):
            state[g0 + gi] = jnp.zeros(state.shape[1:], F32)
            winx[g0 + gi, 0:8, :] = jnp.zeros((8, gw), F32)
            winb[g0 + gi, 0:8, :] = jnp.zeros((8, n), F32)
            winc[g0 + gi, 0:8, :] = jnp.zeros((8, n), F32)

    acsc = acsc_ref[...]
    last = acsc[cs - 1:cs, :]
    e1 = jnp.exp2(acsc)
    wte = dtc_ref[...] * jnp.exp2(last - acsc)
    dec = jnp.broadcast_to(jnp.exp2(last), (16, LANES))
    hi, lo = _split_bf16(jnp.concatenate([e1, wte, dec], axis=0))
    factors = jnp.concatenate([hi, lo], axis=1)

    ri = lax.broadcasted_iota(jnp.int32, (cs, cs), 0)
    ci = lax.broadcasted_iota(jnp.int32, (cs, cs), 1)
    causal = ri >= ci
    lane = lax.broadcasted_iota(jnp.int32, (cs, LANES), 1)

    def conv_silu(u, win, g, w, bias):
        win[g, 8:8 + cs, :] = u
        acc = u * w[SSM_CONV - 1:SSM_CONV, :] + bias
        for s in range(1, SSM_CONV):
            acc = acc + win[g, 8 - s:8 - s + cs, :] * w[SSM_CONV - 1 - s:SSM_CONV - s, :]
        win[g, 0:8, :] = u[cs - 8:cs, :]
        return acc * _sigmoid(acc)

    for gi in range(gps):
        g = g0 + gi
        wide = slice(gi * gw, (gi + 1) * gw)
        narrow = slice(gi * n, (gi + 1) * n)
        xc = conv_silu(xs_ref[:, wide].astype(F32), winx, g, cwx_ref[:, wide], cbx_ref[:, wide])
        bc = conv_silu(b_ref[:, narrow].astype(F32), winb, g, cwb_ref[:, narrow], cbb_ref[:, narrow])
        cc = conv_silu(c_ref[:, narrow].astype(F32), winc, g, cwc_ref[:, narrow], cbc_ref[:, narrow])
        bt_b = bc.T.astype(BF16)
        cc_b = cc.astype(BF16)
        cb = _dot(cc_b, bt_b)

        ex = _dot(factors, e_ref[gi])
        e1_e = ex[0:cs, :]
        wte_e = ex[cs:2 * cs, :]
        dec_e = ex[2 * cs:2 * cs + 1, :]

        acsr = acsr_ref[gi]
        dtr = dtr_ref[gi]
        pieces = []
        for m in range(HEADS_PER_GROUP // 2):
            mats = []
            for k in (2 * m, 2 * m + 1):
                acs_j = jnp.broadcast_to(acsr[k:k + 1, :], (cs, cs))
                seg = acs_j.T - acs_j
                decay = jnp.exp2(jnp.where(causal, seg, -jnp.inf))
                dt_j = jnp.broadcast_to(dtr[k:k + 1, :], (cs, cs))
                mats.append((cb * decay * dt_j).astype(BF16))
            xp = xc[:, m * LANES:(m + 1) * LANES]
            x_lo = jnp.where(lane < SSM_HEAD_DIM, xp, 0.0).astype(BF16)
            x_hi = jnp.where(lane >= SSM_HEAD_DIM, xp, 0.0).astype(BF16)
            pieces.append(_dot(jnp.concatenate(mats, axis=1), jnp.concatenate([x_lo, x_hi], axis=0)))
        y_diag = jnp.concatenate(pieces, axis=1)

        st = state[g]
        y_off = _dot(cc_b, st.astype(BF16)) * e1_e
        y = y_diag + y_off + dsk_ref[:, wide] * xc
        z = z_ref[:, wide].astype(F32)
        y = y * (z * _sigmoid(z))
        ms = jnp.mean(y * y, axis=-1, keepdims=True)
        y_ref[:, wide] = (y * lax.rsqrt(ms + EPS) * nw_ref[:, wide]).astype(BF16)

        xw = (xc * wte_e).astype(BF16)
        state[g] = st * dec_e + _dot(bt_b, xw)


def _ssd(proj, acsc, dtc, acsr, dtr, expand, cw_x, cw_b, cw_c, cb_x, cb_b, cb_c, dskip, norm_w,
         batch, seq):
    t = proj.shape[0]
    cs = SSD_CHUNK
    nc = seq // cs
    gps = SSD_GROUPS_PER_STEP
    gw = gps * GROUP_WIDTH
    n = gps * SSM_STATE

    def rowblk(b, c, g):
        return b * nc + c

    in_specs = [
        pl.BlockSpec((cs, gw), lambda b, c, g: (rowblk(b, c, g), P_Z // gw + g)),
        pl.BlockSpec((cs, gw), lambda b, c, g: (rowblk(b, c, g), P_XS // gw + g)),
        pl.BlockSpec((cs, n), lambda b, c, g: (rowblk(b, c, g), P_B // n + g)),
        pl.BlockSpec((cs, n), lambda b, c, g: (rowblk(b, c, g), P_C // n + g)),
        pl.BlockSpec((cs, LANES), lambda b, c, g: (rowblk(b, c, g), 0)),
        pl.BlockSpec((cs, LANES), lambda b, c, g: (rowblk(b, c, g), 0)),
        pl.BlockSpec((None, gps, HEADS_PER_GROUP, cs), lambda b, c, g: (rowblk(b, c, g), g, 0, 0)),
        pl.BlockSpec((None, gps, HEADS_PER_GROUP, cs), lambda b, c, g: (rowblk(b, c, g), g, 0, 0)),
        pl.BlockSpec((gps, 2 * LANES, GROUP_WIDTH), lambda b, c, g: (g, 0, 0)),
        pl.BlockSpec((SSM_CONV, gw), lambda b, c, g: (0, g)),
        pl.BlockSpec((SSM_CONV, n), lambda b, c, g: (0, g)),
        pl.BlockSpec((SSM_CONV, n), lambda b, c, g: (0, g)),
        pl.BlockSpec((1, gw), lambda b, c, g: (0, g)),
        pl.BlockSpec((1, n), lambda b, c, g: (0, g)),
        pl.BlockSpec((1, n), lambda b, c, g: (0, g)),
        pl.BlockSpec((1, gw), lambda b, c, g: (0, g)),
        pl.BlockSpec((1, gw), lambda b, c, g: (0, g)),
    ]
    return pl.pallas_call(
        functools.partial(_ssd_kernel, gps=gps),
        grid=(batch, nc, SSM_GROUPS // gps),
        in_specs=in_specs,
        out_specs=pl.BlockSpec((cs, gw), lambda b, c, g: (rowblk(b, c, g), g)),
        out_shape=jax.ShapeDtypeStruct((t, SSM_D_INNER), BF16),
        scratch_shapes=[
            pltpu.VMEM((SSM_GROUPS, SSM_STATE, GROUP_WIDTH), F32),
            pltpu.VMEM((SSM_GROUPS, 8 + cs, GROUP_WIDTH), F32),
            pltpu.VMEM((SSM_GROUPS, 8 + cs, SSM_STATE), F32),
            pltpu.VMEM((SSM_GROUPS, 8 + cs, SSM_STATE), F32),
        ],
        compiler_params=_params(("parallel", "arbitrary", "arbitrary")),
        name="ssd_scan",
    )(proj, proj, proj, proj, acsc, dtc, acsr, dtr, expand,
      cw_x, cw_b, cw_c, cb_x, cb_b, cb_c, dskip, norm_w)


def _attn_kernel(lam_ref, q_ref, k_ref, v_ref, cos_ref, s1_ref, s2_ref, qw_ref, kw_ref, sw_ref,
                 o_ref, kst, qp, vt, s_a, s_b, acc, m_scr, l_scr, *, bq, bk, nq, lambda_init):
    nkv = nq * (bq // bk)
    lane = lax.broadcasted_iota(jnp.int32, (bk, LANES), 1)
    ra = lax.broadcasted_iota(jnp.int32, (LANES, LANES), 0) // ATT_QK_DIM
    rb = lax.broadcasted_iota(jnp.int32, (LANES, LANES), 1) // ATT_QK_DIM
    half_mean = jnp.where(ra == rb, 1.0 / ATT_QK_DIM, 0.0).astype(BF16)
    q_scale = (ATT_QK_DIM ** -0.5) * math.log2(math.e)

    def norm_rope(t, w, rows):
        hi, lo = _split_bf16(t * t)
        ms = _dot(hi, half_mean) + _dot(lo, half_mean)
        tn = t * lax.rsqrt(ms + EPS) * w
        return (tn * cos_ref[rows, :] + pltpu.roll(tn, ROT_DIM // 2, 1) * s1_ref[rows, :]
                + pltpu.roll(tn, LANES - ROT_DIM // 2, 1) * s2_ref[rows, :])

    def prep(r2, carry):
        for half in range(bq // bk):
            r = r2 * (bq // bk) + half
            rows = pl.ds(pl.multiple_of(r * bk, bk), bk)
            kb = norm_rope(k_ref[rows, :].astype(F32), kw_ref[...], rows)
            kst[r, 0:bk, :] = jnp.where(lane < ATT_QK_DIM, kb, 0.0).astype(BF16)
            kst[r, bk:2 * bk, :] = jnp.where(lane >= ATT_QK_DIM, kb, 0.0).astype(BF16)
            qb = norm_rope(q_ref[rows, :].astype(F32), qw_ref[...], rows)
            qp[rows, :] = (qb * q_scale).astype(BF16)
            vt[r] = v_ref[rows, :].astype(F32).T.astype(BF16)
        return carry

    lax.fori_loop(0, nq, prep, 0)

    kidx = lax.broadcasted_iota(jnp.int32, (bk, bq), 0)
    qidx = lax.broadcasted_iota(jnp.int32, (bk, bq), 1)
    lam = lam_ref[0]

    def q_rows(qi):
        return pl.ds(pl.multiple_of(qi * bq, bq), bq)

    def produce(dst, qblk, kj):
        dst[...] = lax.dot_general(kst[kj], qblk, (((1,), (1,)), ((), ())), preferred_element_type=F32)

    def consume(src, kj, mask):
        for c in range(2):
            sc = src[c * bk:(c + 1) * bk, :]
            if mask is not None:
                sc = jnp.where(mask, sc, -jnp.inf)
            m_old = m_scr[c]
            m_new = jnp.maximum(m_old, jnp.max(sc, axis=0, keepdims=True))
            alpha = jnp.exp2(m_old - m_new)
            p = jnp.exp2(sc - m_new)
            l_scr[c] = alpha * l_scr[c] + jnp.sum(p, axis=0, keepdims=True)
            acc[c] = alpha * acc[c] + _dot(vt[kj], p.astype(BF16))
            m_scr[c] = m_new

    produce(s_a, qp[q_rows(0), :], 0)

    def q_body(qi, carry):
        qblk = qp[q_rows(qi), :]
        m_scr[...] = jnp.full(m_scr.shape, -jnp.inf, F32)
        l_scr[...] = jnp.zeros(l_scr.shape, F32)
        acc[...] = jnp.zeros(acc.shape, F32)

        def trip(t, c2):
            produce(s_b, qblk, 2 * t + 1)
            consume(s_a, 2 * t, None)
            produce(s_a, qblk, 2 * t + 2)
            consume(s_b, 2 * t + 1, None)
            return c2

        lax.fori_loop(0, qi, trip, 0)
        produce(s_b, qblk, 2 * qi + 1)
        consume(s_a, 2 * qi, kidx <= qidx)
        produce(s_a, qp[q_rows(jnp.minimum(qi + 1, nq - 1)), :], 0)
        consume(s_b, 2 * qi + 1, kidx + bk <= qidx)

        o = acc[0] * (1.0 / l_scr[0]) - acc[1] * (lam / l_scr[1])
        ms = jnp.mean(o * o, axis=0, keepdims=True)
        on = o * lax.rsqrt(ms + SUBLN_EPS) * sw_ref[...] * (1.0 - lambda_init)
        o_ref[q_rows(qi), :] = on.T.astype(BF16)
        return carry

    lax.fori_loop(0, nq, q_body, 0)


def _attention(proj, lam, cos_t, s1_t, s2_t, qw, kw, sw, batch, seq, lambda_init):
    t = proj.shape[0]
    bq = _tile(seq, 512)
    bk = bq // 2
    nq = seq // bq
    nkv = seq // bk
    qoff, koff, voff = P_Q // LANES, P_K // LANES, P_V // LANES
    head_spec = lambda off: pl.BlockSpec((seq, LANES), lambda b, h: (b, off + h))
    tab_spec = pl.BlockSpec((seq, LANES), lambda b, h: (b, 0))
    return pl.pallas_call(
        functools.partial(_attn_kernel, bq=bq, bk=bk, nq=nq, lambda_init=lambda_init),
        grid=(batch, ATT_HEADS),
        in_specs=[
            pl.BlockSpec(memory_space=pltpu.SMEM),
            head_spec(qoff), head_spec(koff), head_spec(voff),
            tab_spec, tab_spec, tab_spec,
            pl.BlockSpec((1, LANES), lambda b, h: (0, 0)),
            pl.BlockSpec((1, LANES), lambda b, h: (0, 0)),
            pl.BlockSpec((ATT_V_DIM, bq), lambda b, h: (0, 0)),
        ],
        out_specs=pl.BlockSpec((seq, LANES), lambda b, h: (b, h)),
        out_shape=jax.ShapeDtypeStruct((t, ATT_HEADS * ATT_V_DIM), BF16),
        scratch_shapes=[
            pltpu.VMEM((nkv, 2 * bk, LANES), BF16),
            pltpu.VMEM((seq, LANES), BF16),
            pltpu.VMEM((nkv, ATT_V_DIM, bk), BF16),
            pltpu.VMEM((2 * bk, bq), F32),
            pltpu.VMEM((2 * bk, bq), F32),
            pltpu.VMEM((2, ATT_V_DIM, bq), F32),
            pltpu.VMEM((2, 1, bq), F32),
            pltpu.VMEM((2, 1, bq), F32),
        ],
        compiler_params=_params(("parallel", "arbitrary")),
        name="diff_attention",
    )(lam, proj, proj, proj, cos_t, s1_t, s2_t, qw, kw, sw)


def _merge_kernel(ys_ref, oa_ref, wbs_ref, wba_ref, gs_ref, ga_ref, o_ref):
    a = _dot(ys_ref[...], wbs_ref[...])
    b = _dot(oa_ref[...], wba_ref[...])
    gs = _sigmoid(gs_ref[...].astype(F32))
    ga = _sigmoid(ga_ref[...].astype(F32))
    o_ref[...] = (gs * a + ga * b).astype(BF16)


def _merge(y_ssm, o_att, w_bs, w_ba, proj):
    t = y_ssm.shape[0]
    tm = _tile(t, 512)
    tn = 1024
    return pl.pallas_call(
        _merge_kernel,
        grid=(t // tm, D_MODEL // tn),
        in_specs=[
            pl.BlockSpec((tm, SSM_D_INNER), lambda i, j: (i, 0)),
            pl.BlockSpec((tm, D_MODEL), lambda i, j: (i, 0)),
            pl.BlockSpec((SSM_D_INNER, tn), lambda i, j: (0, j)),
            pl.BlockSpec((D_MODEL, tn), lambda i, j: (0, j)),
            pl.BlockSpec((tm, tn), lambda i, j: (i, P_GS // tn + j)),
            pl.BlockSpec((tm, tn), lambda i, j: (i, P_GA // tn + j)),
        ],
        out_specs=pl.BlockSpec((tm, tn), lambda i, j: (i, j)),
        out_shape=jax.ShapeDtypeStruct((t, D_MODEL), BF16),
        compiler_params=_params(("parallel", "arbitrary")),
        name="branch_merge",
    )(y_ssm, o_att, w_bs, w_ba, proj, proj)


def _out_proj_kernel(mix_ref, wo_ref, x_ref, nw_ref, x1_ref, h_ref):
    x1 = x_ref[...] + _dot(mix_ref[...], wo_ref[...])
    x1_ref[...] = x1
    ms = jnp.mean(x1 * x1, axis=-1, keepdims=True)
    h_ref[...] = (x1 * lax.rsqrt(ms + EPS) * nw_ref[...]).astype(BF16)


def _out_proj(mixed, w_o, x2, norm_w):
    t = x2.shape[0]
    tm = _tile(t, 512)
    row = lambda i: (i, 0)
    const = lambda i: (0, 0)
    return pl.pallas_call(
        _out_proj_kernel,
        grid=(t // tm,),
        in_specs=[
            pl.BlockSpec((tm, D_MODEL), row),
            pl.BlockSpec((D_MODEL, D_MODEL), const),
            pl.BlockSpec((tm, D_MODEL), row),
            pl.BlockSpec((1, D_MODEL), const),
        ],
        out_specs=[pl.BlockSpec((tm, D_MODEL), row), pl.BlockSpec((tm, D_MODEL), row)],
        out_shape=[jax.ShapeDtypeStruct((t, D_MODEL), F32), jax.ShapeDtypeStruct((t, D_MODEL), BF16)],
        compiler_params=_params(("parallel",)),
        name="out_proj",
    )(mixed, w_o, x2, norm_w)


def _ffn_kernel(h_ref, wg_ref, wu_ref, wd_ref, x_ref, o_ref, acc):
    j = pl.program_id(1)
    h = h_ref[...]
    gate = _dot(h, wg_ref[...])
    up = _dot(h, wu_ref[...])
    part = _dot((gate * _sigmoid(gate) * up).astype(BF16), wd_ref[...])

    @pl.when(j == 0)
    def _():
        acc[...] = part

    @pl.when(j > 0)
    def _():
        acc[...] += part

    @pl.when(j == pl.num_programs(1) - 1)
    def _():
        o_ref[...] = x_ref[...] + acc[...]


def _ffn(h2, w_gu, w_d, x1):
    t = h2.shape[0]
    tm = _tile(t, 512)
    th = 512
    nh = FFN_HIDDEN // th
    return pl.pallas_call(
        _ffn_kernel,
        grid=(t // tm, nh),
        in_specs=[
            pl.BlockSpec((tm, D_MODEL), lambda i, j: (i, 0)),
            pl.BlockSpec((D_MODEL, th), lambda i, j: (0, j)),
            pl.BlockSpec((D_MODEL, th), lambda i, j: (0, nh + j)),
            pl.BlockSpec((th, D_MODEL), lambda i, j: (j, 0)),
            pl.BlockSpec((tm, D_MODEL), lambda i, j: (i, 0)),
        ],
        out_specs=pl.BlockSpec((tm, D_MODEL), lambda i, j: (i, 0)),
        out_shape=jax.ShapeDtypeStruct((t, D_MODEL), F32),
        scratch_shapes=[pltpu.VMEM((tm, D_MODEL), F32)],
        compiler_params=_params(("parallel", "arbitrary")),
        name="ffn",
    )(h2, w_gu, w_gu, w_d, x1)


def _ple_kernel(x_ref, nw_ref, wg_ref, p_ref, wp_ref, o_ref):
    x = x_ref[...]
    ms = jnp.mean(x * x, axis=-1, keepdims=True)
    h = (x * lax.rsqrt(ms + EPS) * nw_ref[...]).astype(BF16)
    gate = _sigmoid(_dot(h, wg_ref[...]))
    o_ref[...] = x + gate * _dot(p_ref[...].astype(BF16), wp_ref[...])


def _ple(x2, norm_w, w_gate, p2, w_ple):
    t = x2.shape[0]
    tm = _tile(t, 512)
    row = lambda i: (i, 0)
    const = lambda i: (0, 0)
    return pl.pallas_call(
        _ple_kernel,
        grid=(t // tm,),
        in_specs=[
            pl.BlockSpec((tm, D_MODEL), row),
            pl.BlockSpec((1, D_MODEL), const),
            pl.BlockSpec((D_MODEL, D_MODEL), const),
            pl.BlockSpec((tm, PLE_DIM), row),
            pl.BlockSpec((PLE_DIM, D_MODEL), const),
        ],
        out_specs=pl.BlockSpec((tm, D_MODEL), row),
        out_shape=jax.ShapeDtypeStruct((t, D_MODEL), F32),
        compiler_params=_params(("parallel",)),
        name="ple",
    )(x2, norm_w, w_gate, p2, w_ple)


def _pad_lanes(v, fill=0.0):
    out = jnp.full((1, LANES), fill, F32)
    return out.at[0, :v.shape[0]].set(v.astype(F32))


def _rope_tables(positions):
    half = ROT_DIM // 2
    inv_freq = ROPE_THETA ** (-jnp.arange(0, ROT_DIM, 2, dtype=F32) / ROT_DIM)
    ang = positions.astype(F32)[..., None] * inv_freq
    cos, sin = jnp.cos(ang), jnp.sin(ang)
    b, l = positions.shape
    rest = ATT_QK_DIM - ROT_DIM
    ones = jnp.ones((b, l, rest), F32)
    zeros = jnp.zeros((b, l, rest), F32)
    z8 = jnp.zeros((b, l, half), F32)
    cos_m = jnp.concatenate([cos, cos, ones], axis=-1)
    s1_m = jnp.concatenate([z8, sin, zeros], axis=-1)
    s2_m = jnp.concatenate([-sin, z8, zeros], axis=-1)
    flat = lambda m: jnp.concatenate([m, m], axis=-1).reshape(b * l, LANES)
    return flat(cos_m), flat(s1_m), flat(s2_m)


def kernel(x, p, positions, w_in, mix_norm_w, conv_w, conv_b, dt_bias, a_log, d_skip, ssm_norm_w, q_norm_w, k_norm_w, lambda_q1, lambda_k1, lambda_q2, lambda_k2, subln_w, w_br_ssm, w_br_attn, w_out, ffn_norm_w, w_gate_up, w_down, ple_norm_w, w_ple_gate, w_ple):
    depth = w_in.shape[0]
    batch, seq, _ = x.shape
    t = batch * seq
    nct = t // SSD_CHUNK
    blk = _tile(seq, 512)

    cos_t, s1_t, s2_t = _rope_tables(positions)
    head_of_row = jnp.arange(2 * LANES)[:, None] % LANES
    head_of_col = jnp.arange(GROUP_WIDTH)[None, :] // SSM_HEAD_DIM
    expand = jnp.stack([(head_of_row == g * HEADS_PER_GROUP + head_of_col) for g in range(SSM_GROUPS)]
                       ).astype(BF16)

    xr = x.reshape(t, D_MODEL)
    for i in range(depth):
        lambda_init = 0.8 - 0.6 * math.exp(-0.3 * i)
        wi = w_in[i]
        w_main = jnp.concatenate([wi[:, :OFF_XBC], wi[:, OFF_DT:]], axis=1).astype(BF16)
        w_dt = jnp.pad(wi[:, OFF_XBC:OFF_DT], ((0, 0), (0, LANES - SSM_HEADS))).astype(BF16)

        proj, dt_raw = _in_proj(xr, mix_norm_w[i][None, :], w_main, w_dt)

        a_neg = _pad_lanes(-jnp.exp(a_log[i].astype(F32)) * math.log2(math.e))
        dtc, acsc, dtr, acsr = _dt_prep(dt_raw, _pad_lanes(dt_bias[i]), a_neg)
        grp = lambda m: m[:, :SSM_HEADS, :].reshape(nct, SSM_GROUPS, HEADS_PER_GROUP, SSD_CHUNK)
        cw = conv_w[i].astype(F32)
        cbias = conv_b[i].astype(F32)[None, :]
        y_ssm = _ssd(
            proj, acsc, dtc, grp(acsr), grp(dtr), expand,
            cw[:, :SSM_D_INNER], cw[:, SSM_D_INNER:SSM_D_INNER + SSM_BC], cw[:, SSM_D_INNER + SSM_BC:],
            cbias[:, :SSM_D_INNER], cbias[:, SSM_D_INNER:SSM_D_INNER + SSM_BC], cbias[:, SSM_D_INNER + SSM_BC:],
            jnp.repeat(d_skip[i].astype(F32), SSM_HEAD_DIM)[None, :], ssm_norm_w[i].astype(F32)[None, :],
            batch, seq)

        lam = (jnp.exp(jnp.sum(lambda_q1[i].astype(F32) * lambda_k1[i].astype(F32)))
               - jnp.exp(jnp.sum(lambda_q2[i].astype(F32) * lambda_k2[i].astype(F32))) + lambda_init)
        two = lambda v: jnp.concatenate([v, v]).astype(F32)[None, :]
        sw = jnp.broadcast_to(subln_w[i].astype(F32)[:, None], (ATT_V_DIM, blk))
        o_att = _attention(proj, lam.reshape(1), cos_t, s1_t, s2_t, two(q_norm_w[i]), two(k_norm_w[i]),
                           sw, batch, seq, lambda_init)

        mixed = _merge(y_ssm, o_att, w_br_ssm[i].astype(BF16), w_br_attn[i].astype(BF16), proj)
        x1, h2 = _out_proj(mixed, w_out[i].astype(BF16), xr, ffn_norm_w[i][None, :])
        x2 = _ffn(h2, w_gate_up[i].astype(BF16), w_down[i].astype(BF16), x1)
        xr = _ple(x2, ple_norm_w[i][None, :], w_ple_gate[i].astype(BF16),
                  p[i].reshape(t, PLE_DIM), w_ple[i].astype(BF16))
    return xr.reshape(batch, seq, D_MODEL)
```

```python
import functools
import math

import jax
import jax.numpy as jnp
from jax import lax
from jax.experimental import pallas as pl
from jax.experimental.pallas import tpu as pltpu

F32 = jnp.float32
BF16 = jnp.bfloat16

D_MODEL = 2048
SSM_D_INNER = 4096
SSM_HEAD_DIM = 64
SSM_HEADS = 64
SSM_GROUPS = 8
SSM_STATE = 128
SSM_CONV = 4
SSM_BC = SSM_GROUPS * SSM_STATE
GROUP_WIDTH = SSM_D_INNER // SSM_GROUPS
HEADS_PER_GROUP = SSM_HEADS // SSM_GROUPS
ATT_HEADS = 16
ATT_QK_DIM = 64
ATT_V_DIM = 128
ROPE_THETA = 500000.0
ROT_DIM = 16
FFN_HIDDEN = 5632
PLE_DIM = 256
EPS = 1e-6
SUBLN_EPS = 1e-5

OFF_Z = SSM_D_INNER
OFF_XBC = OFF_Z + SSM_D_INNER + 2 * SSM_BC
OFF_DT = OFF_XBC + SSM_HEADS
P_Z = 0
P_XS = P_Z + SSM_D_INNER
P_B = P_XS + SSM_D_INNER
P_C = P_B + SSM_BC
P_Q = P_C + SSM_BC
P_K = P_Q + ATT_HEADS * 2 * ATT_QK_DIM
P_V = P_K + ATT_HEADS * 2 * ATT_QK_DIM
P_GS = P_V + ATT_HEADS * ATT_V_DIM
P_GA = P_GS + D_MODEL
P_WIDTH = P_GA + D_MODEL

LANES = 128
SSD_CHUNK = 128
SSD_GROUPS_PER_STEP = 8
ATT_Q_BLOCK = 1024
ATT_K_BLOCK = 256
ATT_SUM_ROWS = 16
VMEM_LIMIT = 56 * 1024 * 1024


def _tile(n, want):
    t = min(n, want)
    while n % t:
        t -= 1
    return t


def _params(sem):
    return pltpu.CompilerParams(dimension_semantics=sem, vmem_limit_bytes=VMEM_LIMIT)


def _sigmoid(v):
    return 1.0 / (1.0 + jnp.exp2(v * (-math.log2(math.e))))


def _dot(a, b):
    return jnp.dot(a, b, preferred_element_type=F32)


def _split_bf16(v):
    hi = v.astype(BF16)
    lo = (v - hi.astype(F32)).astype(BF16)
    return hi, lo


def _in_proj_kernel(x_ref, nw_ref, w_ref, wdt_ref, o_ref, dt_ref, h_scr):
    @pl.when(pl.program_id(1) == 0)
    def _():
        x = x_ref[...]
        ms = jnp.mean(x * x, axis=-1, keepdims=True)
        h = (x * lax.rsqrt(ms + EPS) * nw_ref[...]).astype(BF16)
        h_scr[...] = h
        dt_ref[...] = _dot(h, wdt_ref[...])

    o_ref[...] = _dot(h_scr[...], w_ref[...]).astype(BF16)


def _in_proj(x2, norm_w, w_main, w_dt):
    t = x2.shape[0]
    tm = _tile(t, 1024)
    tn = 1024
    return pl.pallas_call(
        _in_proj_kernel,
        grid=(t // tm, P_WIDTH // tn),
        in_specs=[
            pl.BlockSpec((tm, D_MODEL), lambda i, j: (i, 0)),
            pl.BlockSpec((1, D_MODEL), lambda i, j: (0, 0)),
            pl.BlockSpec((D_MODEL, tn), lambda i, j: (0, j)),
            pl.BlockSpec((D_MODEL, LANES), lambda i, j: (0, 0)),
        ],
        out_specs=[
            pl.BlockSpec((tm, tn), lambda i, j: (i, j)),
            pl.BlockSpec((tm, LANES), lambda i, j: (i, 0)),
        ],
        out_shape=[
            jax.ShapeDtypeStruct((t, P_WIDTH), BF16),
            jax.ShapeDtypeStruct((t, LANES), F32),
        ],
        scratch_shapes=[pltpu.VMEM((tm, D_MODEL), BF16)],
        compiler_params=_params(("parallel", "arbitrary")),
        name="in_proj",
    )(x2, norm_w, w_main, w_dt)


def _dt_kernel(raw_ref, bias_ref, a_ref, dtc_ref, acsc_ref, dtr_ref, acsr_ref, *, nchunk):
    row = lax.broadcasted_iota(jnp.int32, (SSD_CHUNK, LANES), 0)
    for c in range(nchunk):
        rows = slice(c * SSD_CHUNK, (c + 1) * SSD_CHUNK)
        v = raw_ref[rows, :] + bias_ref[...]
        dt = jnp.maximum(v, 0.0) + jnp.log(1.0 + jnp.exp(-jnp.abs(v)))
        acs = dt * a_ref[...]
        shift = 1
        while shift < SSD_CHUNK:
            acs = acs + jnp.where(row >= shift, pltpu.roll(acs, shift, 0), 0.0)
            shift *= 2
        dtc_ref[rows, :] = dt
        acsc_ref[rows, :] = acs
        dtr_ref[c] = dt.T
        acsr_ref[c] = acs.T


def _dt_prep(dt_raw, dt_bias, a_neg):
    t = dt_raw.shape[0]
    rb = _tile(t, 1024)
    nchunk = rb // SSD_CHUNK
    nct = t // SSD_CHUNK
    col = jax.ShapeDtypeStruct((t, LANES), F32)
    rowm = jax.ShapeDtypeStruct((nct, LANES, SSD_CHUNK), F32)
    return pl.pallas_call(
        functools.partial(_dt_kernel, nchunk=nchunk),
        grid=(t // rb,),
        in_specs=[
            pl.BlockSpec((rb, LANES), lambda i: (i, 0)),
            pl.BlockSpec((1, LANES), lambda i: (0, 0)),
            pl.BlockSpec((1, LANES), lambda i: (0, 0)),
        ],
        out_specs=[
            pl.BlockSpec((rb, LANES), lambda i: (i, 0)),
            pl.BlockSpec((rb, LANES), lambda i: (i, 0)),
            pl.BlockSpec((nchunk, LANES, SSD_CHUNK), lambda i: (i, 0, 0)),
            pl.BlockSpec((nchunk, LANES, SSD_CHUNK), lambda i: (i, 0, 0)),
        ],
        out_shape=[col, col, rowm, rowm],
        compiler_params=_params(("parallel",)),
        name="dt_prep",
    )(dt_raw, dt_bias, a_neg)


def _ssd_kernel(z_ref, xs_ref, b_ref, c_ref, acsc_ref, dtc_ref, acsr_ref, dtr_ref, e_ref,
                cwx_ref, cwb_ref, cwc_ref, cbx_ref, cbb_ref, cbc_ref, dsk_ref, nw_ref,
                y_ref, state, winx, winb, winc, *, gps):
    cs = SSD_CHUNK
    gw = GROUP_WIDTH
    n = SSM_STATE
    chunk = pl.program_id(1)
    g0 = pl.program_id(2) * gps

    @pl.when(chunk == 0)
    def _():
        for gi in range(gps):
            state[g0 + gi] = jnp.zeros(state.shape[1:], F32)
            winx[g0 + gi, 0:8, :] = jnp.zeros((8, gw), F32)
            winb[g0 + gi, 0:8, :] = jnp.zeros((8, n), F32)
            winc[g0 + gi, 0:8, :] = jnp.zeros((8, n), F32)

    acsc = acsc_ref[...]
    last = acsc[cs - 1:cs, :]
    e1 = jnp.exp2(acsc)
    wte = dtc_ref[...] * jnp.exp2(last - acsc)
    dec = jnp.broadcast_to(jnp.exp2(last), (16, LANES))
    hi, lo = _split_bf16(jnp.concatenate([e1, wte, dec], axis=0))
    factors = jnp.concatenate([hi, lo], axis=1)

    ri = lax.broadcasted_iota(jnp.int32, (cs, cs), 0)
    ci = lax.broadcasted_iota(jnp.int32, (cs, cs), 1)
    causal = ri >= ci
    lane = lax.broadcasted_iota(jnp.int32, (cs, LANES), 1)

    def conv_silu(u, win, g, w, bias):
        win[g, 8:8 + cs, :] = u
        acc = u * w[SSM_CONV - 1:SSM_CONV, :] + bias
        for s in range(1, SSM_CONV):
            acc = acc + win[g, 8 - s:8 - s + cs, :] * w[SSM_CONV - 1 - s:SSM_CONV - s, :]
        win[g, 0:8, :] = u[cs - 8:cs, :]
        return acc * _sigmoid(acc)

    for gi in range(gps):
        g = g0 + gi
        wide = slice(gi * gw, (gi + 1) * gw)
        narrow = slice(gi * n, (gi + 1) * n)
        xc = conv_silu(xs_ref[:, wide].astype(F32), winx, g, cwx_ref[:, wide], cbx_ref[:, wide])
        bc = conv_silu(b_ref[:, narrow].astype(F32), winb, g, cwb_ref[:, narrow], cbb_ref[:, narrow])
        cc = conv_silu(c_ref[:, narrow].astype(F32), winc, g, cwc_ref[:, narrow], cbc_ref[:, narrow])
        bt_b = bc.T.astype(BF16)
        cc_b = cc.astype(BF16)
        cb = _dot(cc_b, bt_b)

        ex = _dot(factors, e_ref[gi])
        e1_e = ex[0:cs, :]
        wte_e = ex[cs:2 * cs, :]
        dec_e = ex[2 * cs:2 * cs + 1, :]

        acsr = acsr_ref[gi]
        dtr = dtr_ref[gi]
        pieces = []
        for m in range(HEADS_PER_GROUP // 2):
            mats = []
            for k in (2 * m, 2 * m + 1):
                acs_j = jnp.broadcast_to(acsr[k:k + 1, :], (cs, cs))
                seg = acs_j.T - acs_j
                decay = jnp.exp2(jnp.where(causal, seg, -jnp.inf))
                dt_j = jnp.broadcast_to(dtr[k:k + 1, :], (cs, cs))
                mats.append((cb * decay * dt_j).astype(BF16))
            xp = xc[:, m * LANES:(m + 1) * LANES]
            x_lo = jnp.where(lane < SSM_HEAD_DIM, xp, 0.0).astype(BF16)
            x_hi = jnp.where(lane >= SSM_HEAD_DIM, xp, 0.0).astype(BF16)
            pieces.append(_dot(jnp.concatenate(mats, axis=1), jnp.concatenate([x_lo, x_hi], axis=0)))
        y_diag = jnp.concatenate(pieces, axis=1)

        st = state[g]
        y_off = _dot(cc_b, st.astype(BF16)) * e1_e
        y = y_diag + y_off + dsk_ref[:, wide] * xc
        z = z_ref[:, wide].astype(F32)
        y = y * (z * _sigmoid(z))
        ms = jnp.mean(y * y, axis=-1, keepdims=True)
        y_ref[:, wide] = (y * lax.rsqrt(ms + EPS) * nw_ref[:, wide]).astype(BF16)

        xw = (xc * wte_e).astype(BF16)
        state[g] = st * dec_e + _dot(bt_b, xw)


def _ssd(proj, acsc, dtc, acsr, dtr, expand, cw_x, cw_b, cw_c, cb_x, cb_b, cb_c, dskip, norm_w,
         batch, seq):
    t = proj.shape[0]
    cs = SSD_CHUNK
    nc = seq // cs
    gps = SSD_GROUPS_PER_STEP
    gw = gps * GROUP_WIDTH
    n = gps * SSM_STATE

    def rowblk(b, c, g):
        return b * nc + c

    in_specs = [
        pl.BlockSpec((cs, gw), lambda b, c, g: (rowblk(b, c, g), P_Z // gw + g)),
        pl.BlockSpec((cs, gw), lambda b, c, g: (rowblk(b, c, g), P_XS // gw + g)),
        pl.BlockSpec((cs, n), lambda b, c, g: (rowblk(b, c, g), P_B // n + g)),
        pl.BlockSpec((cs, n), lambda b, c, g: (rowblk(b, c, g), P_C // n + g)),
        pl.BlockSpec((cs, LANES), lambda b, c, g: (rowblk(b, c, g), 0)),
        pl.BlockSpec((cs, LANES), lambda b, c, g: (rowblk(b, c, g), 0)),
        pl.BlockSpec((None, gps, HEADS_PER_GROUP, cs), lambda b, c, g: (rowblk(b, c, g), g, 0, 0)),
        pl.BlockSpec((None, gps, HEADS_PER_GROUP, cs), lambda b, c, g: (rowblk(b, c, g), g, 0, 0)),
        pl.BlockSpec((gps, 2 * LANES, GROUP_WIDTH), lambda b, c, g: (g, 0, 0)),
        pl.BlockSpec((SSM_CONV, gw), lambda b, c, g: (0, g)),
        pl.BlockSpec((SSM_CONV, n), lambda b, c, g: (0, g)),
        pl.BlockSpec((SSM_CONV, n), lambda b, c, g: (0, g)),
        pl.BlockSpec((1, gw), lambda b, c, g: (0, g)),
        pl.BlockSpec((1, n), lambda b, c, g: (0, g)),
        pl.BlockSpec((1, n), lambda b, c, g: (0, g)),
        pl.BlockSpec((1, gw), lambda b, c, g: (0, g)),
        pl.BlockSpec((1, gw), lambda b, c, g: (0, g)),
    ]
    return pl.pallas_call(
        functools.partial(_ssd_kernel, gps=gps),
        grid=(batch, nc, SSM_GROUPS // gps),
        in_specs=in_specs,
        out_specs=pl.BlockSpec((cs, gw), lambda b, c, g: (rowblk(b, c, g), g)),
        out_shape=jax.ShapeDtypeStruct((t, SSM_D_INNER), BF16),
        scratch_shapes=[
            pltpu.VMEM((SSM_GROUPS, SSM_STATE, GROUP_WIDTH), F32),
            pltpu.VMEM((SSM_GROUPS, 8 + cs, GROUP_WIDTH), F32),
            pltpu.VMEM((SSM_GROUPS, 8 + cs, SSM_STATE), F32),
            pltpu.VMEM((SSM_GROUPS, 8 + cs, SSM_STATE), F32),
        ],
        compiler_params=_params(("parallel", "arbitrary", "arbitrary")),
        name="ssd_scan",
    )(proj, proj, proj, proj, acsc, dtc, acsr, dtr, expand,
      cw_x, cw_b, cw_c, cb_x, cb_b, cb_c, dskip, norm_w)


def _attn_kernel(lam_ref, q_ref, k_ref, v_ref, cos_ref, s1_ref, s2_ref, qw_ref, kw_ref, sw_ref,
                 o_ref, kst, qp, vt, s_a, s_b, acc, m_scr, l_scr, *, bq, bk, nq, lambda_init):
    nkv = nq * (bq // bk)
    lane = lax.broadcasted_iota(jnp.int32, (bk, LANES), 1)
    ra = lax.broadcasted_iota(jnp.int32, (LANES, LANES), 0) // ATT_QK_DIM
    rb = lax.broadcasted_iota(jnp.int32, (LANES, LANES), 1) // ATT_QK_DIM
    half_mean = jnp.where(ra == rb, 1.0 / ATT_QK_DIM, 0.0).astype(BF16)
    q_scale = (ATT_QK_DIM ** -0.5) * math.log2(math.e)

    def norm_rope(t, w, rows):
        hi, lo = _split_bf16(t * t)
        ms = _dot(hi, half_mean) + _dot(lo, half_mean)
        tn = t * lax.rsqrt(ms + EPS) * w
        return (tn * cos_ref[rows, :] + pltpu.roll(tn, ROT_DIM // 2, 1) * s1_ref[rows, :]
                + pltpu.roll(tn, LANES - ROT_DIM // 2, 1) * s2_ref[rows, :])

    ones_row = jnp.where(lax.broadcasted_iota(jnp.int32, (ATT_SUM_ROWS, bk), 0) == 0, 1.0, 0.0).astype(BF16)

    def prep(r2, carry):
        for half in range(bq // bk):
            r = r2 * (bq // bk) + half
            rows = pl.ds(pl.multiple_of(r * bk, bk), bk)
            kb = norm_rope(k_ref[rows, :].astype(F32), kw_ref[...], rows)
            kst[r, 0:bk, :] = jnp.where(lane < ATT_QK_DIM, kb, 0.0).astype(BF16)
            kst[r, bk:2 * bk, :] = jnp.where(lane >= ATT_QK_DIM, kb, 0.0).astype(BF16)
            qb = norm_rope(q_ref[rows, :].astype(F32), qw_ref[...], rows)
            qp[rows, :] = (qb * q_scale).astype(BF16)
            vt[r, 0:ATT_V_DIM, :] = v_ref[rows, :].astype(F32).T.astype(BF16)
            vt[r, ATT_V_DIM:, :] = ones_row
        return carry

    lax.fori_loop(0, nq, prep, 0)

    nsub = bq // bk
    bufs = (s_a, s_b)
    kidx = lax.broadcasted_iota(jnp.int32, (bk, bk), 0)
    qidx = lax.broadcasted_iota(jnp.int32, (bk, bk), 1)
    tri_mask = kidx <= qidx
    lam = lam_ref[0]

    def q_rows(qi):
        return pl.ds(pl.multiple_of(qi * bq, bq), bq)

    def produce(dst, qblk, kj, lo):
        dst[:, lo:] = lax.dot_general(kst[kj], qblk[lo:, :], (((1,), (1,)), ((), ())),
                                      preferred_element_type=F32)

    def consume(src, kj, lo, diagonal):
        def logits(c):
            sc = src[c * bk:(c + 1) * bk, lo:]
            if diagonal:
                head = jnp.where(tri_mask, sc[:, :bk], -jnp.inf)
                sc = head if lo + bk == bq else jnp.concatenate([head, sc[:, bk:]], axis=1)
            return sc

        for c in range(2):
            m_old = m_scr[c, :, lo:]
            m_new = jnp.maximum(m_old, jnp.max(logits(c), axis=0, keepdims=True))
            alpha = jnp.exp2(m_old - m_new)
            p = jnp.exp2(logits(c) - m_new)
            pv = _dot(vt[kj], p.astype(BF16))
            l_scr[c, :, lo:] = alpha * l_scr[c, :, lo:] + pv[ATT_V_DIM:ATT_V_DIM + 1, :]
            acc[c, :, lo:] = alpha * acc[c, :, lo:] + pv[0:ATT_V_DIM, :]
            m_scr[c, :, lo:] = m_new

    produce(s_a, qp[q_rows(0), :], 0, 0)

    def q_body(qi, carry):
        qblk = qp[q_rows(qi), :]
        m_scr[...] = jnp.full(m_scr.shape, -jnp.inf, F32)
        l_scr[...] = jnp.zeros(l_scr.shape, F32)
        acc[...] = jnp.zeros(acc.shape, F32)

        def trip(t, c2):
            for j in range(nsub):
                kj = t * nsub + j
                produce(bufs[(j + 1) % 2], qblk, kj + 1, 0)
                consume(bufs[j % 2], kj, 0, False)
            return c2

        lax.fori_loop(0, qi, trip, 0)
        for j in range(nsub):
            kj = qi * nsub + j
            if j + 1 < nsub:
                produce(bufs[(j + 1) % 2], qblk, kj + 1, (j + 1) * bk)
            else:
                produce(bufs[0], qp[q_rows(jnp.minimum(qi + 1, nq - 1)), :], 0, 0)
            consume(bufs[j % 2], kj, j * bk, True)

        o = acc[0] * (1.0 / l_scr[0]) - acc[1] * (lam / l_scr[1])
        ms = jnp.mean(o * o, axis=0, keepdims=True)
        on = o * lax.rsqrt(ms + SUBLN_EPS) * sw_ref[...] * (1.0 - lambda_init)
        o_ref[q_rows(qi), :] = on.T.astype(BF16)
        return carry

    lax.fori_loop(0, nq, q_body, 0)


def _attention(proj, lam, cos_t, s1_t, s2_t, qw, kw, sw, batch, seq, lambda_init):
    t = proj.shape[0]
    bq = _tile(seq, ATT_Q_BLOCK)
    bk = min(ATT_K_BLOCK, bq // 2)
    nq = seq // bq
    nkv = seq // bk
    qoff, koff, voff = P_Q // LANES, P_K // LANES, P_V // LANES
    head_spec = lambda off: pl.BlockSpec((seq, LANES), lambda b, h: (b, off + h))
    tab_spec = pl.BlockSpec((seq, LANES), lambda b, h: (b, 0))
    return pl.pallas_call(
        functools.partial(_attn_kernel, bq=bq, bk=bk, nq=nq, lambda_init=lambda_init),
        grid=(batch, ATT_HEADS),
        in_specs=[
            pl.BlockSpec(memory_space=pltpu.SMEM),
            head_spec(qoff), head_spec(koff), head_spec(voff),
            tab_spec, tab_spec, tab_spec,
            pl.BlockSpec((1, LANES), lambda b, h: (0, 0)),
            pl.BlockSpec((1, LANES), lambda b, h: (0, 0)),
            pl.BlockSpec((ATT_V_DIM, bq), lambda b, h: (0, 0)),
        ],
        out_specs=pl.BlockSpec((seq, LANES), lambda b, h: (b, h)),
        out_shape=jax.ShapeDtypeStruct((t, ATT_HEADS * ATT_V_DIM), BF16),
        scratch_shapes=[
            pltpu.VMEM((nkv, 2 * bk, LANES), BF16),
            pltpu.VMEM((seq, LANES), BF16),
            pltpu.VMEM((nkv, ATT_V_DIM + ATT_SUM_ROWS, bk), BF16),
            pltpu.VMEM((2 * bk, bq), F32),
            pltpu.VMEM((2 * bk, bq), F32),
            pltpu.VMEM((2, ATT_V_DIM, bq), F32),
            pltpu.VMEM((2, 1, bq), F32),
            pltpu.VMEM((2, 1, bq), F32),
        ],
        compiler_params=_params(("parallel", "arbitrary")),
        name="diff_attention",
    )(lam, proj, proj, proj, cos_t, s1_t, s2_t, qw, kw, sw)


def _merge_kernel(ys_ref, oa_ref, wbs_ref, wba_ref, gs_ref, ga_ref, o_ref):
    a = _dot(ys_ref[...], wbs_ref[...])
    b = _dot(oa_ref[...], wba_ref[...])
    gs = _sigmoid(gs_ref[...].astype(F32))
    ga = _sigmoid(ga_ref[...].astype(F32))
    o_ref[...] = (gs * a + ga * b).astype(BF16)


def _merge(y_ssm, o_att, w_bs, w_ba, proj):
    t = y_ssm.shape[0]
    tm = _tile(t, 512)
    tn = 1024
    return pl.pallas_call(
        _merge_kernel,
        grid=(t // tm, D_MODEL // tn),
        in_specs=[
            pl.BlockSpec((tm, SSM_D_INNER), lambda i, j: (i, 0)),
            pl.BlockSpec((tm, D_MODEL), lambda i, j: (i, 0)),
            pl.BlockSpec((SSM_D_INNER, tn), lambda i, j: (0, j)),
            pl.BlockSpec((D_MODEL, tn), lambda i, j: (0, j)),
            pl.BlockSpec((tm, tn), lambda i, j: (i, P_GS // tn + j)),
            pl.BlockSpec((tm, tn), lambda i, j: (i, P_GA // tn + j)),
        ],
        out_specs=pl.BlockSpec((tm, tn), lambda i, j: (i, j)),
        out_shape=jax.ShapeDtypeStruct((t, D_MODEL), BF16),
        compiler_params=_params(("parallel", "arbitrary")),
        name="branch_merge",
    )(y_ssm, o_att, w_bs, w_ba, proj, proj)


def _out_proj_kernel(mix_ref, wo_ref, x_ref, nw_ref, x1_ref, h_ref):
    x1 = x_ref[...] + _dot(mix_ref[...], wo_ref[...])
    x1_ref[...] = x1
    ms = jnp.mean(x1 * x1, axis=-1, keepdims=True)
    h_ref[...] = (x1 * lax.rsqrt(ms + EPS) * nw_ref[...]).astype(BF16)


def _out_proj(mixed, w_o, x2, norm_w):
    t = x2.shape[0]
    tm = _tile(t, 512)
    row = lambda i: (i, 0)
    const = lambda i: (0, 0)
    return pl.pallas_call(
        _out_proj_kernel,
        grid=(t // tm,),
        in_specs=[
            pl.BlockSpec((tm, D_MODEL), row),
            pl.BlockSpec((D_MODEL, D_MODEL), const),
            pl.BlockSpec((tm, D_MODEL), row),
            pl.BlockSpec((1, D_MODEL), const),
        ],
        out_specs=[pl.BlockSpec((tm, D_MODEL), row), pl.BlockSpec((tm, D_MODEL), row)],
        out_shape=[jax.ShapeDtypeStruct((t, D_MODEL), F32), jax.ShapeDtypeStruct((t, D_MODEL), BF16)],
        compiler_params=_params(("parallel",)),
        name="out_proj",
    )(mixed, w_o, x2, norm_w)


def _ffn_kernel(h_ref, wg_ref, wu_ref, wd_ref, x_ref, o_ref, act):
    j = pl.program_id(1)
    nh = pl.num_programs(1) - 1

    def activations():
        h = h_ref[...]
        gate = _dot(h, wg_ref[...])
        up = _dot(h, wu_ref[...])
        return (gate * _sigmoid(gate) * up).astype(BF16)

    @pl.when(j == 0)
    def _():
        o_ref[...] = x_ref[...]
        act[0] = activations()

    @pl.when(jnp.logical_and(j > 0, j < nh))
    def _():
        prev = act[(j + 1) % 2]
        act[j % 2] = activations()
        o_ref[...] += _dot(prev, wd_ref[...])

    @pl.when(j == nh)
    def _():
        o_ref[...] += _dot(act[(j + 1) % 2], wd_ref[...])


def _ffn(h2, w_gu, w_d, x1):
    t = h2.shape[0]
    tm = _tile(t, 512)
    th = 512
    nh = FFN_HIDDEN // th
    last = nh - 1
    return pl.pallas_call(
        _ffn_kernel,
        grid=(t // tm, nh + 1),
        in_specs=[
            pl.BlockSpec((tm, D_MODEL), lambda i, j: (i, 0)),
            pl.BlockSpec((D_MODEL, th), lambda i, j: (0, jnp.minimum(j, last))),
            pl.BlockSpec((D_MODEL, th), lambda i, j: (0, nh + jnp.minimum(j, last))),
            pl.BlockSpec((th, D_MODEL), lambda i, j: (jnp.maximum(j - 1, 0), 0)),
            pl.BlockSpec((tm, D_MODEL), lambda i, j: (i, 0)),
        ],
        out_specs=pl.BlockSpec((tm, D_MODEL), lambda i, j: (i, 0)),
        out_shape=jax.ShapeDtypeStruct((t, D_MODEL), F32),
        scratch_shapes=[pltpu.VMEM((2, tm, th), BF16)],
        compiler_params=_params(("parallel", "arbitrary")),
        name="ffn",
    )(h2, w_gu, w_gu, w_d, x1)


def _ple_kernel(x_ref, nw_ref, wg_ref, p_ref, wp_ref, o_ref):
    x = x_ref[...]
    ms = jnp.mean(x * x, axis=-1, keepdims=True)
    h = (x * lax.rsqrt(ms + EPS) * nw_ref[...]).astype(BF16)
    gate = _sigmoid(_dot(h, wg_ref[...]))
    o_ref[...] = x + gate * _dot(p_ref[...].astype(BF16), wp_ref[...])


def _ple(x2, norm_w, w_gate, p2, w_ple):
    t = x2.shape[0]
    tm = _tile(t, 512)
    row = lambda i: (i, 0)
    const = lambda i: (0, 0)
    return pl.pallas_call(
        _ple_kernel,
        grid=(t // tm,),
        in_specs=[
            pl.BlockSpec((tm, D_MODEL), row),
            pl.BlockSpec((1, D_MODEL), const),
            pl.BlockSpec((D_MODEL, D_MODEL), const),
            pl.BlockSpec((tm, PLE_DIM), row),
            pl.BlockSpec((PLE_DIM, D_MODEL), const),
        ],
        out_specs=pl.BlockSpec((tm, D_MODEL), row),
        out_shape=jax.ShapeDtypeStruct((t, D_MODEL), F32),
        compiler_params=_params(("parallel",)),
        name="ple",
    )(x2, norm_w, w_gate, p2, w_ple)


def _pad_lanes(v, fill=0.0):
    out = jnp.full((1, LANES), fill, F32)
    return out.at[0, :v.shape[0]].set(v.astype(F32))


def _rope_tables(positions):
    half = ROT_DIM // 2
    inv_freq = ROPE_THETA ** (-jnp.arange(0, ROT_DIM, 2, dtype=F32) / ROT_DIM)
    ang = positions.astype(F32)[..., None] * inv_freq
    cos, sin = jnp.cos(ang), jnp.sin(ang)
    b, l = positions.shape
    rest = ATT_QK_DIM - ROT_DIM
    ones = jnp.ones((b, l, rest), F32)
    zeros = jnp.zeros((b, l, rest), F32)
    z8 = jnp.zeros((b, l, half), F32)
    cos_m = jnp.concatenate([cos, cos, ones], axis=-1)
    s1_m = jnp.concatenate([z8, sin, zeros], axis=-1)
    s2_m = jnp.concatenate([-sin, z8, zeros], axis=-1)
    flat = lambda m: jnp.concatenate([m, m], axis=-1).reshape(b * l, LANES)
    return flat(cos_m), flat(s1_m), flat(s2_m)


def kernel(x, p, positions, w_in, mix_norm_w, conv_w, conv_b, dt_bias, a_log, d_skip, ssm_norm_w, q_norm_w, k_norm_w, lambda_q1, lambda_k1, lambda_q2, lambda_k2, subln_w, w_br_ssm, w_br_attn, w_out, ffn_norm_w, w_gate_up, w_down, ple_norm_w, w_ple_gate, w_ple):
    depth = w_in.shape[0]
    batch, seq, _ = x.shape
    t = batch * seq
    nct = t // SSD_CHUNK
    blk = _tile(seq, ATT_Q_BLOCK)

    cos_t, s1_t, s2_t = _rope_tables(positions)
    head_of_row = jnp.arange(2 * LANES)[:, None] % LANES
    head_of_col = jnp.arange(GROUP_WIDTH)[None, :] // SSM_HEAD_DIM
    expand = jnp.stack([(head_of_row == g * HEADS_PER_GROUP + head_of_col) for g in range(SSM_GROUPS)]
                       ).astype(BF16)

    xr = x.reshape(t, D_MODEL)
    for i in range(depth):
        lambda_init = 0.8 - 0.6 * math.exp(-0.3 * i)
        wi = w_in[i]
        w_main = jnp.concatenate([wi[:, :OFF_XBC], wi[:, OFF_DT:]], axis=1).astype(BF16)
        w_dt = jnp.pad(wi[:, OFF_XBC:OFF_DT], ((0, 0), (0, LANES - SSM_HEADS))).astype(BF16)

        proj, dt_raw = _in_proj(xr, mix_norm_w[i][None, :], w_main, w_dt)

        a_neg = _pad_lanes(-jnp.exp(a_log[i].astype(F32)) * math.log2(math.e))
        dtc, acsc, dtr, acsr = _dt_prep(dt_raw, _pad_lanes(dt_bias[i]), a_neg)
        grp = lambda m: m[:, :SSM_HEADS, :].reshape(nct, SSM_GROUPS, HEADS_PER_GROUP, SSD_CHUNK)
        cw = conv_w[i].astype(F32)
        cbias = conv_b[i].astype(F32)[None, :]
        y_ssm = _ssd(
            proj, acsc, dtc, grp(acsr), grp(dtr), expand,
            cw[:, :SSM_D_INNER], cw[:, SSM_D_INNER:SSM_D_INNER + SSM_BC], cw[:, SSM_D_INNER + SSM_BC:],
            cbias[:, :SSM_D_INNER], cbias[:, SSM_D_INNER:SSM_D_INNER + SSM_BC], cbias[:, SSM_D_INNER + SSM_BC:],
            jnp.repeat(d_skip[i].astype(F32), SSM_HEAD_DIM)[None, :], ssm_norm_w[i].astype(F32)[None, :],
            batch, seq)

        lam = (jnp.exp(jnp.sum(lambda_q1[i].astype(F32) * lambda_k1[i].astype(F32)))
               - jnp.exp(jnp.sum(lambda_q2[i].astype(F32) * lambda_k2[i].astype(F32))) + lambda_init)
        two = lambda v: jnp.concatenate([v, v]).astype(F32)[None, :]
        sw = jnp.broadcast_to(subln_w[i].astype(F32)[:, None], (ATT_V_DIM, blk))
        o_att = _attention(proj, lam.reshape(1), cos_t, s1_t, s2_t, two(q_norm_w[i]), two(k_norm_w[i]),
                           sw, batch, seq, lambda_init)

        mixed = _merge(y_ssm, o_att, w_br_ssm[i].astype(BF16), w_br_attn[i].astype(BF16), proj)
        x1, h2 = _out_proj(mixed, w_out[i].astype(BF16), xr, ffn_norm_w[i][None, :])
        x2 = _ffn(h2, w_gate_up[i].astype(BF16), w_down[i].astype(BF16), x1)
        xr = _ple(x2, ple_norm_w[i][None, :], w_ple_gate[i].astype(BF16),
                  p[i].reshape(t, PLE_DIM), w_ple[i].astype(BF16))
    return xr.reshape(batch, seq, D_MODEL)
```

```python
import functools
import math

import jax
import jax.numpy as jnp
from jax import lax
from jax.experimental import pallas as pl
from jax.experimental.pallas import tpu as pltpu

F32 = jnp.float32
BF16 = jnp.bfloat16

D_MODEL = 2048
SSM_D_INNER = 4096
SSM_HEAD_DIM = 64
SSM_HEADS = 64
SSM_GROUPS = 8
SSM_STATE = 128
SSM_CONV = 4
SSM_BC = SSM_GROUPS * SSM_STATE
GROUP_WIDTH = SSM_D_INNER // SSM_GROUPS
HEADS_PER_GROUP = SSM_HEADS // SSM_GROUPS
ATT_HEADS = 16
ATT_QK_DIM = 64
ATT_V_DIM = 128
ROPE_THETA = 500000.0
ROT_DIM = 16
FFN_HIDDEN = 5632
PLE_DIM = 256
EPS = 1e-6
SUBLN_EPS = 1e-5

OFF_Z = SSM_D_INNER
OFF_XBC = OFF_Z + SSM_D_INNER + 2 * SSM_BC
OFF_DT = OFF_XBC + SSM_HEADS
P_Z = 0
P_XS = P_Z + SSM_D_INNER
P_B = P_XS + SSM_D_INNER
P_C = P_B + SSM_BC
P_Q = P_C + SSM_BC
P_K = P_Q + ATT_HEADS * 2 * ATT_QK_DIM
P_V = P_K + ATT_HEADS * 2 * ATT_QK_DIM
P_GS = P_V + ATT_HEADS * ATT_V_DIM
P_GA = P_GS + D_MODEL
P_WIDTH = P_GA + D_MODEL

LANES = 128
SSD_CHUNK = 128
SSD_GROUPS_PER_STEP = 8
ATT_Q_BLOCK = 1024
ATT_K_BLOCK = 256
ATT_SUM_ROWS = 16
VMEM_LIMIT = 56 * 1024 * 1024


def _tile(n, want):
    t = min(n, want)
    while n % t:
        t -= 1
    return t


def _params(sem):
    return pltpu.CompilerParams(dimension_semantics=sem, vmem_limit_bytes=VMEM_LIMIT)


def _sigmoid(v):
    return 1.0 / (1.0 + jnp.exp2(v * (-math.log2(math.e))))


def _dot(a, b):
    return jnp.dot(a, b, preferred_element_type=F32)


def _split_bf16(v):
    hi = v.astype(BF16)
    lo = (v - hi.astype(F32)).astype(BF16)
    return hi, lo


def _in_proj_kernel(x_ref, nw_ref, w_ref, wdt_ref, o_ref, dt_ref, h_scr):
    @pl.when(pl.program_id(1) == 0)
    def _():
        x = x_ref[...]
        ms = jnp.mean(x * x, axis=-1, keepdims=True)
        h = (x * lax.rsqrt(ms + EPS) * nw_ref[...]).astype(BF16)
        h_scr[...] = h
        dt_ref[...] = _dot(h, wdt_ref[...])

    o_ref[...] = _dot(h_scr[...], w_ref[...]).astype(BF16)


def _col_tiles(w, tn):
    k, n = w.shape
    return w.astype(BF16).reshape(k, n // tn, tn).transpose(1, 0, 2)


def _in_proj(x2, norm_w, w_tiles, w_dt):
    t = x2.shape[0]
    tm = _tile(t, 1024)
    tn = w_tiles.shape[2]
    return pl.pallas_call(
        _in_proj_kernel,
        grid=(t // tm, P_WIDTH // tn),
        in_specs=[
            pl.BlockSpec((tm, D_MODEL), lambda i, j: (i, 0)),
            pl.BlockSpec((1, D_MODEL), lambda i, j: (0, 0)),
            pl.BlockSpec((None, D_MODEL, tn), lambda i, j: (j, 0, 0)),
            pl.BlockSpec((D_MODEL, LANES), lambda i, j: (0, 0)),
        ],
        out_specs=[
            pl.BlockSpec((tm, tn), lambda i, j: (i, j)),
            pl.BlockSpec((tm, LANES), lambda i, j: (i, 0)),
        ],
        out_shape=[
            jax.ShapeDtypeStruct((t, P_WIDTH), BF16),
            jax.ShapeDtypeStruct((t, LANES), F32),
        ],
        scratch_shapes=[pltpu.VMEM((tm, D_MODEL), BF16)],
        compiler_params=_params(("parallel", "arbitrary")),
        name="in_proj",
    )(x2, norm_w, w_tiles, w_dt)


def _dt_kernel(raw_ref, bias_ref, a_ref, dtc_ref, acsc_ref, dtr_ref, acsr_ref, *, nchunk):
    row = lax.broadcasted_iota(jnp.int32, (SSD_CHUNK, LANES), 0)
    for c in range(nchunk):
        rows = slice(c * SSD_CHUNK, (c + 1) * SSD_CHUNK)
        v = raw_ref[rows, :] + bias_ref[...]
        dt = jnp.maximum(v, 0.0) + jnp.log(1.0 + jnp.exp(-jnp.abs(v)))
        acs = dt * a_ref[...]
        shift = 1
        while shift < SSD_CHUNK:
            acs = acs + jnp.where(row >= shift, pltpu.roll(acs, shift, 0), 0.0)
            shift *= 2
        dtc_ref[rows, :] = dt
        acsc_ref[rows, :] = acs
        dtr_ref[c] = dt.T
        acsr_ref[c] = acs.T


def _dt_prep(dt_raw, dt_bias, a_neg):
    t = dt_raw.shape[0]
    rb = _tile(t, 1024)
    nchunk = rb // SSD_CHUNK
    nct = t // SSD_CHUNK
    col = jax.ShapeDtypeStruct((t, LANES), F32)
    rowm = jax.ShapeDtypeStruct((nct, LANES, SSD_CHUNK), F32)
    return pl.pallas_call(
        functools.partial(_dt_kernel, nchunk=nchunk),
        grid=(t // rb,),
        in_specs=[
            pl.BlockSpec((rb, LANES), lambda i: (i, 0)),
            pl.BlockSpec((1, LANES), lambda i: (0, 0)),
            pl.BlockSpec((1, LANES), lambda i: (0, 0)),
        ],
        out_specs=[
            pl.BlockSpec((rb, LANES), lambda i: (i, 0)),
            pl.BlockSpec((rb, LANES), lambda i: (i, 0)),
            pl.BlockSpec((nchunk, LANES, SSD_CHUNK), lambda i: (i, 0, 0)),
            pl.BlockSpec((nchunk, LANES, SSD_CHUNK), lambda i: (i, 0, 0)),
        ],
        out_shape=[col, col, rowm, rowm],
        compiler_params=_params(("parallel",)),
        name="dt_prep",
    )(dt_raw, dt_bias, a_neg)


def _ssd_kernel(z_ref, xs_ref, b_ref, c_ref, acsc_ref, dtc_ref, acsr_ref, dtr_ref, e_ref,
                cwx_ref, cwb_ref, cwc_ref, cbx_ref, cbb_ref, cbc_ref, dsk_ref, nw_ref,
                y_ref, state, winx, winb, winc, *, gps):
    cs = SSD_CHUNK
    gw = GROUP_WIDTH
    n = SSM_STATE
    chunk = pl.program_id(1)
    g0 = pl.program_id(2) * gps

    @pl.when(chunk == 0)
    def _():
        for gi in range(gps):
            state[g0 + gi] = jnp.zeros(state.shape[1:], F32)
            winx[g0 + gi, 0:8, :] = jnp.zeros((8, gw), F32)
            winb[g0 + gi, 0:8, :] = jnp.zeros((8, n), F32)
            winc[g0 + gi, 0:8, :] = jnp.zeros((8, n), F32)

    acsc = acsc_ref[...]
    last = acsc[cs - 1:cs, :]
    e1 = jnp.exp2(acsc)
    wte = dtc_ref[...] * jnp.exp2(last - acsc)
    dec = jnp.broadcast_to(jnp.exp2(last), (16, LANES))
    hi, lo = _split_bf16(jnp.concatenate([e1, wte, dec], axis=0))
    factors = jnp.concatenate([hi, lo], axis=1)

    ri = lax.broadcasted_iota(jnp.int32, (cs, cs), 0)
    ci = lax.broadcasted_iota(jnp.int32, (cs, cs), 1)
    causal = ri >= ci
    lane = lax.broadcasted_iota(jnp.int32, (cs, LANES), 1)

    def conv_silu(u, win, g, w, bias):
        win[g, 8:8 + cs, :] = u
        acc = u * w[SSM_CONV - 1:SSM_CONV, :] + bias
        for s in range(1, SSM_CONV):
            acc = acc + win[g, 8 - s:8 - s + cs, :] * w[SSM_CONV - 1 - s:SSM_CONV - s, :]
        win[g, 0:8, :] = u[cs - 8:cs, :]
        return acc * _sigmoid(acc)

    for gi in range(gps):
        g = g0 + gi
        wide = slice(gi * gw, (gi + 1) * gw)
        narrow = slice(gi * n, (gi + 1) * n)
        xc = conv_silu(xs_ref[:, wide].astype(F32), winx, g, cwx_ref[:, wide], cbx_ref[:, wide])
        bc = conv_silu(b_ref[:, narrow].astype(F32), winb, g, cwb_ref[:, narrow], cbb_ref[:, narrow])
        cc = conv_silu(c_ref[:, narrow].astype(F32), winc, g, cwc_ref[:, narrow], cbc_ref[:, narrow])
        bt_b = bc.T.astype(BF16)
        cc_b = cc.astype(BF16)
        cb = _dot(cc_b, bt_b)

        ex = _dot(factors, e_ref[gi])
        e1_e = ex[0:cs, :]
        wte_e = ex[cs:2 * cs, :]
        dec_e = ex[2 * cs:2 * cs + 1, :]

        acsr = acsr_ref[gi]
        dtr = dtr_ref[gi]
        pieces = []
        for m in range(HEADS_PER_GROUP // 2):
            mats = []
            for k in (2 * m, 2 * m + 1):
                acs_j = jnp.broadcast_to(acsr[k:k + 1, :], (cs, cs))
                seg = acs_j.T - acs_j
                decay = jnp.exp2(jnp.where(causal, seg, -jnp.inf))
                dt_j = jnp.broadcast_to(dtr[k:k + 1, :], (cs, cs))
                mats.append((cb * decay * dt_j).astype(BF16))
            xp = xc[:, m * LANES:(m + 1) * LANES]
            x_lo = jnp.where(lane < SSM_HEAD_DIM, xp, 0.0).astype(BF16)
            x_hi = jnp.where(lane >= SSM_HEAD_DIM, xp, 0.0).astype(BF16)
            pieces.append(_dot(jnp.concatenate(mats, axis=1), jnp.concatenate([x_lo, x_hi], axis=0)))
        y_diag = jnp.concatenate(pieces, axis=1)

        st = state[g]
        y_off = _dot(cc_b, st.astype(BF16)) * e1_e
        y = y_diag + y_off + dsk_ref[:, wide] * xc
        z = z_ref[:, wide].astype(F32)
        y = y * (z * _sigmoid(z))
        ms = jnp.mean(y * y, axis=-1, keepdims=True)
        y_ref[:, wide] = (y * lax.rsqrt(ms + EPS) * nw_ref[:, wide]).astype(BF16)

        xw = (xc * wte_e).astype(BF16)
        state[g] = st * dec_e + _dot(bt_b, xw)


def _ssd(proj, acsc, dtc, acsr, dtr, expand, cw_x, cw_b, cw_c, cb_x, cb_b, cb_c, dskip, norm_w,
         batch, seq):
    t = proj.shape[0]
    cs = SSD_CHUNK
    nc = seq // cs
    gps = SSD_GROUPS_PER_STEP
    gw = gps * GROUP_WIDTH
    n = gps * SSM_STATE

    def rowblk(b, c, g):
        return b * nc + c

    in_specs = [
        pl.BlockSpec((cs, gw), lambda b, c, g: (rowblk(b, c, g), P_Z // gw + g)),
        pl.BlockSpec((cs, gw), lambda b, c, g: (rowblk(b, c, g), P_XS // gw + g)),
        pl.BlockSpec((cs, n), lambda b, c, g: (rowblk(b, c, g), P_B // n + g)),
        pl.BlockSpec((cs, n), lambda b, c, g: (rowblk(b, c, g), P_C // n + g)),
        pl.BlockSpec((cs, LANES), lambda b, c, g: (rowblk(b, c, g), 0)),
        pl.BlockSpec((cs, LANES), lambda b, c, g: (rowblk(b, c, g), 0)),
        pl.BlockSpec((None, gps, HEADS_PER_GROUP, cs), lambda b, c, g: (rowblk(b, c, g), g, 0, 0)),
        pl.BlockSpec((None, gps, HEADS_PER_GROUP, cs), lambda b, c, g: (rowblk(b, c, g), g, 0, 0)),
        pl.BlockSpec((gps, 2 * LANES, GROUP_WIDTH), lambda b, c, g: (g, 0, 0)),
        pl.BlockSpec((SSM_CONV, gw), lambda b, c, g: (0, g)),
        pl.BlockSpec((SSM_CONV, n), lambda b, c, g: (0, g)),
        pl.BlockSpec((SSM_CONV, n), lambda b, c, g: (0, g)),
        pl.BlockSpec((1, gw), lambda b, c, g: (0, g)),
        pl.BlockSpec((1, n), lambda b, c, g: (0, g)),
        pl.BlockSpec((1, n), lambda b, c, g: (0, g)),
        pl.BlockSpec((1, gw), lambda b, c, g: (0, g)),
        pl.BlockSpec((1, gw), lambda b, c, g: (0, g)),
    ]
    return pl.pallas_call(
        functools.partial(_ssd_kernel, gps=gps),
        grid=(batch, nc, SSM_GROUPS // gps),
        in_specs=in_specs,
        out_specs=pl.BlockSpec((cs, gw), lambda b, c, g: (rowblk(b, c, g), g)),
        out_shape=jax.ShapeDtypeStruct((t, SSM_D_INNER), BF16),
        scratch_shapes=[
            pltpu.VMEM((SSM_GROUPS, SSM_STATE, GROUP_WIDTH), F32),
            pltpu.VMEM((SSM_GROUPS, 8 + cs, GROUP_WIDTH), F32),
            pltpu.VMEM((SSM_GROUPS, 8 + cs, SSM_STATE), F32),
            pltpu.VMEM((SSM_GROUPS, 8 + cs, SSM_STATE), F32),
        ],
        compiler_params=_params(("parallel", "arbitrary", "arbitrary")),
        name="ssd_scan",
    )(proj, proj, proj, proj, acsc, dtc, acsr, dtr, expand,
      cw_x, cw_b, cw_c, cb_x, cb_b, cb_c, dskip, norm_w)


def _attn_kernel(lam_ref, q_ref, k_ref, v_ref, cos_ref, s1_ref, s2_ref, qw_ref, kw_ref, sw_ref,
                 o_ref, kst, qp, vt, s_a, s_b, acc, m_scr, l_scr, *, bq, bk, nq, lambda_init):
    nkv = nq * (bq // bk)
    lane = lax.broadcasted_iota(jnp.int32, (bk, LANES), 1)
    ra = lax.broadcasted_iota(jnp.int32, (LANES, LANES), 0) // ATT_QK_DIM
    rb = lax.broadcasted_iota(jnp.int32, (LANES, LANES), 1) // ATT_QK_DIM
    half_mean = jnp.where(ra == rb, 1.0 / ATT_QK_DIM, 0.0).astype(BF16)
    q_scale = (ATT_QK_DIM ** -0.5) * math.log2(math.e)

    def norm_rope(t, w, rows):
        ms = _dot((t * t).astype(BF16), half_mean)
        tn = t * lax.rsqrt(ms + EPS) * w
        return (tn * cos_ref[rows, :] + pltpu.roll(tn, ROT_DIM // 2, 1) * s1_ref[rows, :]
                + pltpu.roll(tn, LANES - ROT_DIM // 2, 1) * s2_ref[rows, :])

    ones_row = jnp.where(lax.broadcasted_iota(jnp.int32, (ATT_SUM_ROWS, bk), 0) == 0, 1.0, 0.0).astype(BF16)

    def prep(r2, carry):
        for half in range(bq // bk):
            r = r2 * (bq // bk) + half
            rows = pl.ds(pl.multiple_of(r * bk, bk), bk)
            kb = norm_rope(k_ref[rows, :].astype(F32), kw_ref[...], rows)
            kst[r, 0:bk, :] = jnp.where(lane < ATT_QK_DIM, kb, 0.0).astype(BF16)
            kst[r, bk:2 * bk, :] = jnp.where(lane >= ATT_QK_DIM, kb, 0.0).astype(BF16)
            qb = norm_rope(q_ref[rows, :].astype(F32), qw_ref[...], rows)
            qp[rows, :] = (qb * q_scale).astype(BF16)
            vt[r, 0:ATT_V_DIM, :] = v_ref[rows, :].T
            vt[r, ATT_V_DIM:, :] = ones_row
        return carry

    lax.fori_loop(0, nq, prep, 0)

    nsub = bq // bk
    bufs = (s_a, s_b)
    kidx = lax.broadcasted_iota(jnp.int32, (bk, bk), 0)
    qidx = lax.broadcasted_iota(jnp.int32, (bk, bk), 1)
    tri_mask = kidx <= qidx
    lam = lam_ref[0]

    def q_rows(qi):
        return pl.ds(pl.multiple_of(qi * bq, bq), bq)

    def produce(dst, qblk, kj, lo):
        dst[:, lo:] = lax.dot_general(kst[kj], qblk[lo:, :], (((1,), (1,)), ((), ())),
                                      preferred_element_type=F32)

    def consume(src, kj, lo, diagonal):
        def logits(c):
            sc = src[c * bk:(c + 1) * bk, lo:]
            if diagonal:
                head = jnp.where(tri_mask, sc[:, :bk], -jnp.inf)
                sc = head if lo + bk == bq else jnp.concatenate([head, sc[:, bk:]], axis=1)
            return sc

        for c in range(2):
            m_old = m_scr[c, :, lo:]
            m_new = jnp.maximum(m_old, jnp.max(logits(c), axis=0, keepdims=True))
            alpha = jnp.exp2(m_old - m_new)
            p = jnp.exp2(logits(c) - m_new)
            pv = _dot(vt[kj], p.astype(BF16))
            l_scr[c, :, lo:] = alpha * l_scr[c, :, lo:] + pv[ATT_V_DIM:ATT_V_DIM + 1, :]
            acc[c, :, lo:] = alpha * acc[c, :, lo:] + pv[0:ATT_V_DIM, :]
            m_scr[c, :, lo:] = m_new

    produce(s_a, qp[q_rows(0), :], 0, 0)

    def q_body(qi, carry):
        qblk = qp[q_rows(qi), :]
        m_scr[...] = jnp.full(m_scr.shape, -jnp.inf, F32)
        l_scr[...] = jnp.zeros(l_scr.shape, F32)
        acc[...] = jnp.zeros(acc.shape, F32)

        def trip(t, c2):
            for j in range(nsub):
                kj = t * nsub + j
                produce(bufs[(j + 1) % 2], qblk, kj + 1, 0)
                consume(bufs[j % 2], kj, 0, False)
            return c2

        lax.fori_loop(0, qi, trip, 0)
        for j in range(nsub):
            kj = qi * nsub + j
            if j + 1 < nsub:
                produce(bufs[(j + 1) % 2], qblk, kj + 1, (j + 1) * bk)
            else:
                produce(bufs[0], qp[q_rows(jnp.minimum(qi + 1, nq - 1)), :], 0, 0)
            consume(bufs[j % 2], kj, j * bk, True)

        o = acc[0] * (1.0 / l_scr[0]) - acc[1] * (lam / l_scr[1])
        ms = jnp.mean(o * o, axis=0, keepdims=True)
        on = o * lax.rsqrt(ms + SUBLN_EPS) * sw_ref[...] * (1.0 - lambda_init)
        o_ref[q_rows(qi), :] = on.T.astype(BF16)
        return carry

    lax.fori_loop(0, nq, q_body, 0)


def _attention(proj, lam, cos_t, s1_t, s2_t, qw, kw, sw, batch, seq, lambda_init):
    t = proj.shape[0]
    bq = _tile(seq, ATT_Q_BLOCK)
    bk = min(ATT_K_BLOCK, bq // 2)
    nq = seq // bq
    nkv = seq // bk
    qoff, koff, voff = P_Q // LANES, P_K // LANES, P_V // LANES
    head_spec = lambda off: pl.BlockSpec((seq, LANES), lambda b, h: (b, off + h))
    tab_spec = pl.BlockSpec((seq, LANES), lambda b, h: (b, 0))
    return pl.pallas_call(
        functools.partial(_attn_kernel, bq=bq, bk=bk, nq=nq, lambda_init=lambda_init),
        grid=(batch, ATT_HEADS),
        in_specs=[
            pl.BlockSpec(memory_space=pltpu.SMEM),
            head_spec(qoff), head_spec(koff), head_spec(voff),
            tab_spec, tab_spec, tab_spec,
            pl.BlockSpec((1, LANES), lambda b, h: (0, 0)),
            pl.BlockSpec((1, LANES), lambda b, h: (0, 0)),
            pl.BlockSpec((ATT_V_DIM, bq), lambda b, h: (0, 0)),
        ],
        out_specs=pl.BlockSpec((seq, LANES), lambda b, h: (b, h)),
        out_shape=jax.ShapeDtypeStruct((t, ATT_HEADS * ATT_V_DIM), BF16),
        scratch_shapes=[
            pltpu.VMEM((nkv, 2 * bk, LANES), BF16),
            pltpu.VMEM((seq, LANES), BF16),
            pltpu.VMEM((nkv, ATT_V_DIM + ATT_SUM_ROWS, bk), BF16),
            pltpu.VMEM((2 * bk, bq), F32),
            pltpu.VMEM((2 * bk, bq), F32),
            pltpu.VMEM((2, ATT_V_DIM, bq), F32),
            pltpu.VMEM((2, 1, bq), F32),
            pltpu.VMEM((2, 1, bq), F32),
        ],
        compiler_params=_params(("parallel", "arbitrary")),
        name="diff_attention",
    )(lam, proj, proj, proj, cos_t, s1_t, s2_t, qw, kw, sw)


def _merge_kernel(ys_ref, oa_ref, wbs_ref, wba_ref, gs_ref, ga_ref, o_ref):
    a = _dot(ys_ref[...], wbs_ref[...])
    b = _dot(oa_ref[...], wba_ref[...])
    gs = _sigmoid(gs_ref[...].astype(F32))
    ga = _sigmoid(ga_ref[...].astype(F32))
    o_ref[...] = (gs * a + ga * b).astype(BF16)


def _merge(y_ssm, o_att, w_bs, w_ba, proj):
    t = y_ssm.shape[0]
    tm = _tile(t, 512)
    tn = w_bs.shape[2]
    return pl.pallas_call(
        _merge_kernel,
        grid=(t // tm, D_MODEL // tn),
        in_specs=[
            pl.BlockSpec((tm, SSM_D_INNER), lambda i, j: (i, 0)),
            pl.BlockSpec((tm, D_MODEL), lambda i, j: (i, 0)),
            pl.BlockSpec((None, SSM_D_INNER, tn), lambda i, j: (j, 0, 0)),
            pl.BlockSpec((None, D_MODEL, tn), lambda i, j: (j, 0, 0)),
            pl.BlockSpec((tm, tn), lambda i, j: (i, P_GS // tn + j)),
            pl.BlockSpec((tm, tn), lambda i, j: (i, P_GA // tn + j)),
        ],
        out_specs=pl.BlockSpec((tm, tn), lambda i, j: (i, j)),
        out_shape=jax.ShapeDtypeStruct((t, D_MODEL), BF16),
        compiler_params=_params(("parallel", "arbitrary")),
        name="branch_merge",
    )(y_ssm, o_att, w_bs, w_ba, proj, proj)


def _out_proj_kernel(mix_ref, wo_ref, x_ref, nw_ref, x1_ref, h_ref):
    x1 = x_ref[...] + _dot(mix_ref[...], wo_ref[...])
    x1_ref[...] = x1
    ms = jnp.mean(x1 * x1, axis=-1, keepdims=True)
    h_ref[...] = (x1 * lax.rsqrt(ms + EPS) * nw_ref[...]).astype(BF16)


def _out_proj(mixed, w_o, x2, norm_w):
    t = x2.shape[0]
    tm = _tile(t, 512)
    row = lambda i: (i, 0)
    const = lambda i: (0, 0)
    return pl.pallas_call(
        _out_proj_kernel,
        grid=(t // tm,),
        in_specs=[
            pl.BlockSpec((tm, D_MODEL), row),
            pl.BlockSpec((D_MODEL, D_MODEL), const),
            pl.BlockSpec((tm, D_MODEL), row),
            pl.BlockSpec((1, D_MODEL), const),
        ],
        out_specs=[pl.BlockSpec((tm, D_MODEL), row), pl.BlockSpec((tm, D_MODEL), row)],
        out_shape=[jax.ShapeDtypeStruct((t, D_MODEL), F32), jax.ShapeDtypeStruct((t, D_MODEL), BF16)],
        compiler_params=_params(("parallel",)),
        name="out_proj",
    )(mixed, w_o, x2, norm_w)


def _ffn_kernel(h_ref, wg_ref, wu_ref, wd_ref, x_ref, o_ref, act):
    j = pl.program_id(1)
    nh = pl.num_programs(1) - 1

    def activations():
        h = h_ref[...]
        gate = _dot(h, wg_ref[...])
        up = _dot(h, wu_ref[...])
        return (gate * _sigmoid(gate) * up).astype(BF16)

    @pl.when(j == 0)
    def _():
        o_ref[...] = x_ref[...]
        act[0] = activations()

    @pl.when(jnp.logical_and(j > 0, j < nh))
    def _():
        prev = act[(j + 1) % 2]
        act[j % 2] = activations()
        o_ref[...] += _dot(prev, wd_ref[...])

    @pl.when(j == nh)
    def _():
        o_ref[...] += _dot(act[(j + 1) % 2], wd_ref[...])


def _ffn(h2, w_gu, w_d, x1):
    t = h2.shape[0]
    tm = _tile(t, 512)
    th = w_gu.shape[2]
    nh = FFN_HIDDEN // th
    last = nh - 1
    return pl.pallas_call(
        _ffn_kernel,
        grid=(t // tm, nh + 1),
        in_specs=[
            pl.BlockSpec((tm, D_MODEL), lambda i, j: (i, 0)),
            pl.BlockSpec((None, D_MODEL, th), lambda i, j: (jnp.minimum(j, last), 0, 0)),
            pl.BlockSpec((None, D_MODEL, th), lambda i, j: (nh + jnp.minimum(j, last), 0, 0)),
            pl.BlockSpec((th, D_MODEL), lambda i, j: (jnp.maximum(j - 1, 0), 0)),
            pl.BlockSpec((tm, D_MODEL), lambda i, j: (i, 0)),
        ],
        out_specs=pl.BlockSpec((tm, D_MODEL), lambda i, j: (i, 0)),
        out_shape=jax.ShapeDtypeStruct((t, D_MODEL), F32),
        scratch_shapes=[pltpu.VMEM((2, tm, th), BF16)],
        compiler_params=_params(("parallel", "arbitrary")),
        name="ffn",
    )(h2, w_gu, w_gu, w_d, x1)


def _ple_kernel(x_ref, nw_ref, wg_ref, p_ref, wp_ref, o_ref):
    x = x_ref[...]
    ms = jnp.mean(x * x, axis=-1, keepdims=True)
    h = (x * lax.rsqrt(ms + EPS) * nw_ref[...]).astype(BF16)
    gate = _sigmoid(_dot(h, wg_ref[...]))
    o_ref[...] = x + gate * _dot(p_ref[...].astype(BF16), wp_ref[...])


def _ple(x2, norm_w, w_gate, p2, w_ple):
    t = x2.shape[0]
    tm = _tile(t, 512)
    row = lambda i: (i, 0)
    const = lambda i: (0, 0)
    return pl.pallas_call(
        _ple_kernel,
        grid=(t // tm,),
        in_specs=[
            pl.BlockSpec((tm, D_MODEL), row),
            pl.BlockSpec((1, D_MODEL), const),
            pl.BlockSpec((D_MODEL, D_MODEL), const),
            pl.BlockSpec((tm, PLE_DIM), row),
            pl.BlockSpec((PLE_DIM, D_MODEL), const),
        ],
        out_specs=pl.BlockSpec((tm, D_MODEL), row),
        out_shape=jax.ShapeDtypeStruct((t, D_MODEL), F32),
        compiler_params=_params(("parallel",)),
        name="ple",
    )(x2, norm_w, w_gate, p2, w_ple)


def _pad_lanes(v, fill=0.0):
    out = jnp.full((1, LANES), fill, F32)
    return out.at[0, :v.shape[0]].set(v.astype(F32))


def _rope_tables(positions):
    half = ROT_DIM // 2
    inv_freq = ROPE_THETA ** (-jnp.arange(0, ROT_DIM, 2, dtype=F32) / ROT_DIM)
    ang = positions.astype(F32)[..., None] * inv_freq
    cos, sin = jnp.cos(ang), jnp.sin(ang)
    b, l = positions.shape
    rest = ATT_QK_DIM - ROT_DIM
    ones = jnp.ones((b, l, rest), F32)
    zeros = jnp.zeros((b, l, rest), F32)
    z8 = jnp.zeros((b, l, half), F32)
    cos_m = jnp.concatenate([cos, cos, ones], axis=-1)
    s1_m = jnp.concatenate([z8, sin, zeros], axis=-1)
    s2_m = jnp.concatenate([-sin, z8, zeros], axis=-1)
    flat = lambda m: jnp.concatenate([m, m], axis=-1).reshape(b * l, LANES)
    return flat(cos_m), flat(s1_m), flat(s2_m)


def kernel(x, p, positions, w_in, mix_norm_w, conv_w, conv_b, dt_bias, a_log, d_skip, ssm_norm_w, q_norm_w, k_norm_w, lambda_q1, lambda_k1, lambda_q2, lambda_k2, subln_w, w_br_ssm, w_br_attn, w_out, ffn_norm_w, w_gate_up, w_down, ple_norm_w, w_ple_gate, w_ple):
    depth = w_in.shape[0]
    batch, seq, _ = x.shape
    t = batch * seq
    nct = t // SSD_CHUNK
    blk = _tile(seq, ATT_Q_BLOCK)

    cos_t, s1_t, s2_t = _rope_tables(positions)
    head_of_row = jnp.arange(2 * LANES)[:, None] % LANES
    head_of_col = jnp.arange(GROUP_WIDTH)[None, :] // SSM_HEAD_DIM
    expand = jnp.stack([(head_of_row == g * HEADS_PER_GROUP + head_of_col) for g in range(SSM_GROUPS)]
                       ).astype(BF16)

    xr = x.reshape(t, D_MODEL)
    for i in range(depth):
        lambda_init = 0.8 - 0.6 * math.exp(-0.3 * i)
        wi = w_in[i]
        w_main = _col_tiles(jnp.concatenate([wi[:, :OFF_XBC], wi[:, OFF_DT:]], axis=1), 2048)
        w_dt = jnp.pad(wi[:, OFF_XBC:OFF_DT], ((0, 0), (0, LANES - SSM_HEADS))).astype(BF16)

        proj, dt_raw = _in_proj(xr, mix_norm_w[i][None, :], w_main, w_dt)

        a_neg = _pad_lanes(-jnp.exp(a_log[i].astype(F32)) * math.log2(math.e))
        dtc, acsc, dtr, acsr = _dt_prep(dt_raw, _pad_lanes(dt_bias[i]), a_neg)
        grp = lambda m: m[:, :SSM_HEADS, :].reshape(nct, SSM_GROUPS, HEADS_PER_GROUP, SSD_CHUNK)
        cw = conv_w[i].astype(F32)
        cbias = conv_b[i].astype(F32)[None, :]
        y_ssm = _ssd(
            proj, acsc, dtc, grp(acsr), grp(dtr), expand,
            cw[:, :SSM_D_INNER], cw[:, SSM_D_INNER:SSM_D_INNER + SSM_BC], cw[:, SSM_D_INNER + SSM_BC:],
            cbias[:, :SSM_D_INNER], cbias[:, SSM_D_INNER:SSM_D_INNER + SSM_BC], cbias[:, SSM_D_INNER + SSM_BC:],
            jnp.repeat(d_skip[i].astype(F32), SSM_HEAD_DIM)[None, :], ssm_norm_w[i].astype(F32)[None, :],
            batch, seq)

        lam = (jnp.exp(jnp.sum(lambda_q1[i].astype(F32) * lambda_k1[i].astype(F32)))
               - jnp.exp(jnp.sum(lambda_q2[i].astype(F32) * lambda_k2[i].astype(F32))) + lambda_init)
        two = lambda v: jnp.concatenate([v, v]).astype(F32)[None, :]
        sw = jnp.broadcast_to(subln_w[i].astype(F32)[:, None], (ATT_V_DIM, blk))
        o_att = _attention(proj, lam.reshape(1), cos_t, s1_t, s2_t, two(q_norm_w[i]), two(k_norm_w[i]),
                           sw, batch, seq, lambda_init)

        mixed = _merge(y_ssm, o_att, _col_tiles(w_br_ssm[i], 1024), _col_tiles(w_br_attn[i], 1024), proj)
        x1, h2 = _out_proj(mixed, w_out[i].astype(BF16), xr, ffn_norm_w[i][None, :])
        x2 = _ffn(h2, _col_tiles(w_gate_up[i], 512), w_down[i].astype(BF16), x1)
        xr = _ple(x2, ple_norm_w[i][None, :], w_ple_gate[i].astype(BF16),
                  p[i].reshape(t, PLE_DIM), w_ple[i].astype(BF16))
    return xr.reshape(batch, seq, D_MODEL)
```

```python
import functools
import math

import jax
import jax.numpy as jnp
from jax import lax
from jax.experimental import pallas as pl
from jax.experimental.pallas import tpu as pltpu

F32 = jnp.float32
BF16 = jnp.bfloat16

D_MODEL = 2048
SSM_D_INNER = 4096
SSM_HEAD_DIM = 64
SSM_HEADS = 64
SSM_GROUPS = 8
SSM_STATE = 128
SSM_CONV = 4
SSM_BC = SSM_GROUPS * SSM_STATE
GROUP_WIDTH = SSM_D_INNER // SSM_GROUPS
HEADS_PER_GROUP = SSM_HEADS // SSM_GROUPS
ATT_HEADS = 16
ATT_QK_DIM = 64
ATT_V_DIM = 128
ROPE_THETA = 500000.0
ROT_DIM = 16
FFN_HIDDEN = 5632
PLE_DIM = 256
EPS = 1e-6
SUBLN_EPS = 1e-5

OFF_Z = SSM_D_INNER
OFF_XBC = OFF_Z + SSM_D_INNER + 2 * SSM_BC
OFF_DT = OFF_XBC + SSM_HEADS
P_Z = 0
P_XS = P_Z + SSM_D_INNER
P_B = P_XS + SSM_D_INNER
P_C = P_B + SSM_BC
P_Q = P_C + SSM_BC
P_K = P_Q + ATT_HEADS * 2 * ATT_QK_DIM
P_V = P_K + ATT_HEADS * 2 * ATT_QK_DIM
P_GS = P_V + ATT_HEADS * ATT_V_DIM
P_GA = P_GS + D_MODEL
P_WIDTH = P_GA + D_MODEL

LANES = 128
SSD_CHUNK = 128
SSD_GROUPS_PER_STEP = 8
ATT_Q_BLOCK = 1024
ATT_K_BLOCK = 256
ATT_SUM_ROWS = 16
VMEM_LIMIT = 56 * 1024 * 1024


def _tile(n, want):
    t = min(n, want)
    while n % t:
        t -= 1
    return t


def _params(sem):
    return pltpu.CompilerParams(dimension_semantics=sem, vmem_limit_bytes=VMEM_LIMIT)


def _sigmoid(v):
    return 1.0 / (1.0 + jnp.exp2(v * (-math.log2(math.e))))


def _dot(a, b):
    return jnp.dot(a, b, preferred_element_type=F32)


def _split_bf16(v):
    hi = v.astype(BF16)
    lo = (v - hi.astype(F32)).astype(BF16)
    return hi, lo


def _in_proj_kernel(x_ref, nw_ref, w_ref, wdt_ref, o_ref, dt_ref, h_scr):
    @pl.when(pl.program_id(1) == 0)
    def _():
        x = x_ref[...]
        ms = jnp.mean(x * x, axis=-1, keepdims=True)
        h = (x * lax.rsqrt(ms + EPS) * nw_ref[...]).astype(BF16)
        h_scr[...] = h
        dt_ref[...] = _dot(h, wdt_ref[...])

    o_ref[...] = _dot(h_scr[...], w_ref[...]).astype(BF16)


def _in_proj_tiles(wi, tn):
    assert OFF_XBC % tn == 0 and P_WIDTH % tn == 0
    starts = [c if c < OFF_XBC else c + SSM_HEADS for c in range(0, P_WIDTH, tn)]
    return jnp.stack([wi[:, s:s + tn] for s in starts]).astype(BF16)


def _in_proj(x2, norm_w, w_tiles, w_dt):
    t = x2.shape[0]
    tm = _tile(t, 1024)
    tn = w_tiles.shape[2]
    return pl.pallas_call(
        _in_proj_kernel,
        grid=(t // tm, P_WIDTH // tn),
        in_specs=[
            pl.BlockSpec((tm, D_MODEL), lambda i, j: (i, 0)),
            pl.BlockSpec((1, D_MODEL), lambda i, j: (0, 0)),
            pl.BlockSpec((None, D_MODEL, tn), lambda i, j: (j, 0, 0)),
            pl.BlockSpec((D_MODEL, LANES), lambda i, j: (0, 0)),
        ],
        out_specs=[
            pl.BlockSpec((tm, tn), lambda i, j: (i, j)),
            pl.BlockSpec((tm, LANES), lambda i, j: (i, 0)),
        ],
        out_shape=[
            jax.ShapeDtypeStruct((t, P_WIDTH), BF16),
            jax.ShapeDtypeStruct((t, LANES), F32),
        ],
        scratch_shapes=[pltpu.VMEM((tm, D_MODEL), BF16)],
        compiler_params=_params(("parallel", "arbitrary")),
        name="in_proj",
    )(x2, norm_w, w_tiles, w_dt)


def _dt_kernel(raw_ref, bias_ref, a_ref, dtc_ref, acsc_ref, dtr_ref, acsr_ref, *, nchunk):
    row = lax.broadcasted_iota(jnp.int32, (SSD_CHUNK, LANES), 0)
    for c in range(nchunk):
        rows = slice(c * SSD_CHUNK, (c + 1) * SSD_CHUNK)
        v = raw_ref[rows, :] + bias_ref[...]
        dt = jnp.maximum(v, 0.0) + jnp.log(1.0 + jnp.exp(-jnp.abs(v)))
        acs = dt * a_ref[...]
        shift = 1
        while shift < SSD_CHUNK:
            acs = acs + jnp.where(row >= shift, pltpu.roll(acs, shift, 0), 0.0)
            shift *= 2
        dtc_ref[rows, :] = dt
        acsc_ref[rows, :] = acs
        dtr_ref[c] = dt.T
        acsr_ref[c] = acs.T


def _dt_prep(dt_raw, dt_bias, a_neg):
    t = dt_raw.shape[0]
    rb = _tile(t, 1024)
    nchunk = rb // SSD_CHUNK
    nct = t // SSD_CHUNK
    col = jax.ShapeDtypeStruct((t, LANES), F32)
    rowm = jax.ShapeDtypeStruct((nct, LANES, SSD_CHUNK), F32)
    return pl.pallas_call(
        functools.partial(_dt_kernel, nchunk=nchunk),
        grid=(t // rb,),
        in_specs=[
            pl.BlockSpec((rb, LANES), lambda i: (i, 0)),
            pl.BlockSpec((1, LANES), lambda i: (0, 0)),
            pl.BlockSpec((1, LANES), lambda i: (0, 0)),
        ],
        out_specs=[
            pl.BlockSpec((rb, LANES), lambda i: (i, 0)),
            pl.BlockSpec((rb, LANES), lambda i: (i, 0)),
            pl.BlockSpec((nchunk, LANES, SSD_CHUNK), lambda i: (i, 0, 0)),
            pl.BlockSpec((nchunk, LANES, SSD_CHUNK), lambda i: (i, 0, 0)),
        ],
        out_shape=[col, col, rowm, rowm],
        compiler_params=_params(("parallel",)),
        name="dt_prep",
    )(dt_raw, dt_bias, a_neg)


def _ssd_kernel(z_ref, xs_ref, b_ref, c_ref, acsc_ref, dtc_ref, acsr_ref, dtr_ref, e_ref,
                cwx_ref, cwb_ref, cwc_ref, cbx_ref, cbb_ref, cbc_ref, dsk_ref, nw_ref,
                y_ref, state, winx, winb, winc, *, gps):
    cs = SSD_CHUNK
    gw = GROUP_WIDTH
    n = SSM_STATE
    chunk = pl.program_id(1)
    g0 = pl.program_id(2) * gps

    @pl.when(chunk == 0)
    def _():
        for gi in range(gps):
            state[g0 + gi] = jnp.zeros(state.shape[1:], F32)
            winx[g0 + gi, 0:8, :] = jnp.zeros((8, gw), F32)
            winb[g0 + gi, 0:8, :] = jnp.zeros((8, n), F32)
            winc[g0 + gi, 0:8, :] = jnp.zeros((8, n), F32)

    acsc = acsc_ref[...]
    last = acsc[cs - 1:cs, :]
    e1 = jnp.exp2(acsc)
    wte = dtc_ref[...] * jnp.exp2(last - acsc)
    dec = jnp.broadcast_to(jnp.exp2(last), (16, LANES))
    hi, lo = _split_bf16(jnp.concatenate([e1, wte, dec], axis=0))
    factors = jnp.concatenate([hi, lo], axis=1)

    ri = lax.broadcasted_iota(jnp.int32, (cs, cs), 0)
    ci = lax.broadcasted_iota(jnp.int32, (cs, cs), 1)
    causal = ri >= ci
    lane = lax.broadcasted_iota(jnp.int32, (cs, LANES), 1)

    def conv_silu(u, win, g, w, bias):
        win[g, 8:8 + cs, :] = u
        acc = u * w[SSM_CONV - 1:SSM_CONV, :] + bias
        for s in range(1, SSM_CONV):
            acc = acc + win[g, 8 - s:8 - s + cs, :] * w[SSM_CONV - 1 - s:SSM_CONV - s, :]
        win[g, 0:8, :] = u[cs - 8:cs, :]
        return acc * _sigmoid(acc)

    for gi in range(gps):
        g = g0 + gi
        wide = slice(gi * gw, (gi + 1) * gw)
        narrow = slice(gi * n, (gi + 1) * n)
        xc = conv_silu(xs_ref[:, wide].astype(F32), winx, g, cwx_ref[:, wide], cbx_ref[:, wide])
        bc = conv_silu(b_ref[:, narrow].astype(F32), winb, g, cwb_ref[:, narrow], cbb_ref[:, narrow])
        cc = conv_silu(c_ref[:, narrow].astype(F32), winc, g, cwc_ref[:, narrow], cbc_ref[:, narrow])
        bt_b = bc.T.astype(BF16)
        cc_b = cc.astype(BF16)
        cb = _dot(cc_b, bt_b)

        ex = _dot(factors, e_ref[gi])
        e1_e = ex[0:cs, :]
        wte_e = ex[cs:2 * cs, :]
        dec_e = ex[2 * cs:2 * cs + 1, :]

        acsr = acsr_ref[gi]
        dtr = dtr_ref[gi]
        pieces = []
        for m in range(HEADS_PER_GROUP // 2):
            mats = []
            for k in (2 * m, 2 * m + 1):
                acs_j = jnp.broadcast_to(acsr[k:k + 1, :], (cs, cs))
                seg = acs_j.T - acs_j
                decay = jnp.exp2(jnp.where(causal, seg, -jnp.inf))
                dt_j = jnp.broadcast_to(dtr[k:k + 1, :], (cs, cs))
                mats.append((cb * decay * dt_j).astype(BF16))
            xp = xc[:, m * LANES:(m + 1) * LANES]
            x_lo = jnp.where(lane < SSM_HEAD_DIM, xp, 0.0).astype(BF16)
            x_hi = jnp.where(lane >= SSM_HEAD_DIM, xp, 0.0).astype(BF16)
            pieces.append(_dot(jnp.concatenate(mats, axis=1), jnp.concatenate([x_lo, x_hi], axis=0)))
        y_diag = jnp.concatenate(pieces, axis=1)

        st = state[g]
        y_off = _dot(cc_b, st.astype(BF16)) * e1_e
        y = y_diag + y_off + dsk_ref[:, wide] * xc
        z = z_ref[:, wide].astype(F32)
        y = y * (z * _sigmoid(z))
        ms = jnp.mean(y * y, axis=-1, keepdims=True)
        y_ref[:, wide] = (y * lax.rsqrt(ms + EPS) * nw_ref[:, wide]).astype(BF16)

        xw = (xc * wte_e).astype(BF16)
        state[g] = st * dec_e + _dot(bt_b, xw)


def _ssd(proj, acsc, dtc, acsr, dtr, expand, cw_x, cw_b, cw_c, cb_x, cb_b, cb_c, dskip, norm_w,
         batch, seq):
    t = proj.shape[0]
    cs = SSD_CHUNK
    nc = seq // cs
    gps = SSD_GROUPS_PER_STEP
    gw = gps * GROUP_WIDTH
    n = gps * SSM_STATE

    def rowblk(b, c, g):
        return b * nc + c

    in_specs = [
        pl.BlockSpec((cs, gw), lambda b, c, g: (rowblk(b, c, g), P_Z // gw + g)),
        pl.BlockSpec((cs, gw), lambda b, c, g: (rowblk(b, c, g), P_XS // gw + g)),
        pl.BlockSpec((cs, n), lambda b, c, g: (rowblk(b, c, g), P_B // n + g)),
        pl.BlockSpec((cs, n), lambda b, c, g: (rowblk(b, c, g), P_C // n + g)),
        pl.BlockSpec((cs, LANES), lambda b, c, g: (rowblk(b, c, g), 0)),
        pl.BlockSpec((cs, LANES), lambda b, c, g: (rowblk(b, c, g), 0)),
        pl.BlockSpec((None, gps, HEADS_PER_GROUP, cs), lambda b, c, g: (rowblk(b, c, g), g, 0, 0)),
        pl.BlockSpec((None, gps, HEADS_PER_GROUP, cs), lambda b, c, g: (rowblk(b, c, g), g, 0, 0)),
        pl.BlockSpec((gps, 2 * LANES, GROUP_WIDTH), lambda b, c, g: (g, 0, 0)),
        pl.BlockSpec((SSM_CONV, gw), lambda b, c, g: (0, g)),
        pl.BlockSpec((SSM_CONV, n), lambda b, c, g: (0, g)),
        pl.BlockSpec((SSM_CONV, n), lambda b, c, g: (0, g)),
        pl.BlockSpec((1, gw), lambda b, c, g: (0, g)),
        pl.BlockSpec((1, n), lambda b, c, g: (0, g)),
        pl.BlockSpec((1, n), lambda b, c, g: (0, g)),
        pl.BlockSpec((1, gw), lambda b, c, g: (0, g)),
        pl.BlockSpec((1, gw), lambda b, c, g: (0, g)),
    ]
    return pl.pallas_call(
        functools.partial(_ssd_kernel, gps=gps),
        grid=(batch, nc, SSM_GROUPS // gps),
        in_specs=in_specs,
        out_specs=pl.BlockSpec((cs, gw), lambda b, c, g: (rowblk(b, c, g), g)),
        out_shape=jax.ShapeDtypeStruct((t, SSM_D_INNER), BF16),
        scratch_shapes=[
            pltpu.VMEM((SSM_GROUPS, SSM_STATE, GROUP_WIDTH), F32),
            pltpu.VMEM((SSM_GROUPS, 8 + cs, GROUP_WIDTH), F32),
            pltpu.VMEM((SSM_GROUPS, 8 + cs, SSM_STATE), F32),
            pltpu.VMEM((SSM_GROUPS, 8 + cs, SSM_STATE), F32),
        ],
        compiler_params=_params(("parallel", "arbitrary", "arbitrary")),
        name="ssd_scan",
    )(proj, proj, proj, proj, acsc, dtc, acsr, dtr, expand,
      cw_x, cw_b, cw_c, cb_x, cb_b, cb_c, dskip, norm_w)


def _attn_kernel(lam_ref, q_ref, k_ref, v_ref, cos_ref, s1_ref, s2_ref, qw_ref, kw_ref, sw_ref,
                 o_ref, kst, qp, vt, s_a, s_b, acc, m_scr, l_scr, *, bq, bk, nq, lambda_init):
    nkv = nq * (bq // bk)
    lane = lax.broadcasted_iota(jnp.int32, (bk, LANES), 1)
    ra = lax.broadcasted_iota(jnp.int32, (LANES, LANES), 0) // ATT_QK_DIM
    rb = lax.broadcasted_iota(jnp.int32, (LANES, LANES), 1) // ATT_QK_DIM
    half_mean = jnp.where(ra == rb, 1.0 / ATT_QK_DIM, 0.0).astype(BF16)
    q_scale = (ATT_QK_DIM ** -0.5) * math.log2(math.e)

    def norm_rope(t, w, rows):
        ms = _dot((t * t).astype(BF16), half_mean)
        tn = t * lax.rsqrt(ms + EPS) * w
        return (tn * cos_ref[rows, :] + pltpu.roll(tn, ROT_DIM // 2, 1) * s1_ref[rows, :]
                + pltpu.roll(tn, LANES - ROT_DIM // 2, 1) * s2_ref[rows, :])

    ones_row = jnp.where(lax.broadcasted_iota(jnp.int32, (ATT_SUM_ROWS, bk), 0) == 0, 1.0, 0.0).astype(BF16)

    def prep(r2, carry):
        for half in range(bq // bk):
            r = r2 * (bq // bk) + half
            rows = pl.ds(pl.multiple_of(r * bk, bk), bk)
            kb = norm_rope(k_ref[rows, :].astype(F32), kw_ref[...], rows)
            kst[r, 0:bk, :] = jnp.where(lane < ATT_QK_DIM, kb, 0.0).astype(BF16)
            kst[r, bk:2 * bk, :] = jnp.where(lane >= ATT_QK_DIM, kb, 0.0).astype(BF16)
            qb = norm_rope(q_ref[rows, :].astype(F32), qw_ref[...], rows)
            qp[rows, :] = (qb * q_scale).astype(BF16)
            vt[r, 0:ATT_V_DIM, :] = v_ref[rows, :].T
            vt[r, ATT_V_DIM:, :] = ones_row
        return carry

    lax.fori_loop(0, nq, prep, 0)

    nsub = bq // bk
    bufs = (s_a, s_b)
    kidx = lax.broadcasted_iota(jnp.int32, (bk, bk), 0)
    qidx = lax.broadcasted_iota(jnp.int32, (bk, bk), 1)
    tri_mask = kidx <= qidx
    lam = lam_ref[0]

    def q_rows(qi):
        return pl.ds(pl.multiple_of(qi * bq, bq), bq)

    def produce(dst, qblk, kj, lo):
        dst[:, lo:] = lax.dot_general(kst[kj], qblk[lo:, :], (((1,), (1,)), ((), ())),
                                      preferred_element_type=F32)

    def consume(src, kj, lo, diagonal):
        def logits(c):
            sc = src[c * bk:(c + 1) * bk, lo:]
            if diagonal:
                head = jnp.where(tri_mask, sc[:, :bk], -jnp.inf)
                sc = head if lo + bk == bq else jnp.concatenate([head, sc[:, bk:]], axis=1)
            return sc

        for c in range(2):
            m_old = m_scr[c, :, lo:]
            m_new = jnp.maximum(m_old, jnp.max(logits(c), axis=0, keepdims=True))
            alpha = jnp.exp2(m_old - m_new)
            p = jnp.exp2(logits(c) - m_new)
            pv = _dot(vt[kj], p.astype(BF16))
            l_scr[c, :, lo:] = alpha * l_scr[c, :, lo:] + pv[ATT_V_DIM:ATT_V_DIM + 1, :]
            acc[c, :, lo:] = alpha * acc[c, :, lo:] + pv[0:ATT_V_DIM, :]
            m_scr[c, :, lo:] = m_new

    produce(s_a, qp[q_rows(0), :], 0, 0)

    def q_body(qi, carry):
        qblk = qp[q_rows(qi), :]
        m_scr[...] = jnp.full(m_scr.shape, -jnp.inf, F32)
        l_scr[...] = jnp.zeros(l_scr.shape, F32)
        acc[...] = jnp.zeros(acc.shape, F32)

        def trip(t, c2):
            for j in range(nsub):
                kj = t * nsub + j
                produce(bufs[(j + 1) % 2], qblk, kj + 1, 0)
                consume(bufs[j % 2], kj, 0, False)
            return c2

        lax.fori_loop(0, qi, trip, 0)
        for j in range(nsub):
            kj = qi * nsub + j
            if j + 1 < nsub:
                produce(bufs[(j + 1) % 2], qblk, kj + 1, (j + 1) * bk)
            else:
                produce(bufs[0], qp[q_rows(jnp.minimum(qi + 1, nq - 1)), :], 0, 0)
            consume(bufs[j % 2], kj, j * bk, True)

        o = acc[0] * (1.0 / l_scr[0]) - acc[1] * (lam / l_scr[1])
        ms = jnp.mean(o * o, axis=0, keepdims=True)
        on = o * lax.rsqrt(ms + SUBLN_EPS) * sw_ref[...] * (1.0 - lambda_init)
        o_ref[q_rows(qi), :] = on.T.astype(BF16)
        return carry

    lax.fori_loop(0, nq, q_body, 0)


def _attention(proj, lam, cos_t, s1_t, s2_t, qw, kw, sw, batch, seq, lambda_init):
    t = proj.shape[0]
    bq = _tile(seq, ATT_Q_BLOCK)
    bk = min(ATT_K_BLOCK, bq // 2)
    nq = seq // bq
    nkv = seq // bk
    qoff, koff, voff = P_Q // LANES, P_K // LANES, P_V // LANES
    head_spec = lambda off: pl.BlockSpec((seq, LANES), lambda b, h: (b, off + h))
    tab_spec = pl.BlockSpec((seq, LANES), lambda b, h: (b, 0))
    return pl.pallas_call(
        functools.partial(_attn_kernel, bq=bq, bk=bk, nq=nq, lambda_init=lambda_init),
        grid=(batch, ATT_HEADS),
        in_specs=[
            pl.BlockSpec(memory_space=pltpu.SMEM),
            head_spec(qoff), head_spec(koff), head_spec(voff),
            tab_spec, tab_spec, tab_spec,
            pl.BlockSpec((1, LANES), lambda b, h: (0, 0)),
            pl.BlockSpec((1, LANES), lambda b, h: (0, 0)),
            pl.BlockSpec((ATT_V_DIM, bq), lambda b, h: (0, 0)),
        ],
        out_specs=pl.BlockSpec((seq, LANES), lambda b, h: (b, h)),
        out_shape=jax.ShapeDtypeStruct((t, ATT_HEADS * ATT_V_DIM), BF16),
        scratch_shapes=[
            pltpu.VMEM((nkv, 2 * bk, LANES), BF16),
            pltpu.VMEM((seq, LANES), BF16),
            pltpu.VMEM((nkv, ATT_V_DIM + ATT_SUM_ROWS, bk), BF16),
            pltpu.VMEM((2 * bk, bq), F32),
            pltpu.VMEM((2 * bk, bq), F32),
            pltpu.VMEM((2, ATT_V_DIM, bq), F32),
            pltpu.VMEM((2, 1, bq), F32),
            pltpu.VMEM((2, 1, bq), F32),
        ],
        compiler_params=_params(("parallel", "arbitrary")),
        name="diff_attention",
    )(lam, proj, proj, proj, cos_t, s1_t, s2_t, qw, kw, sw)


def _merge_kernel(ys_ref, oa_ref, wbs_ref, wba_ref, gs_ref, ga_ref, o_ref):
    a = _dot(ys_ref[...], wbs_ref[...])
    b = _dot(oa_ref[...], wba_ref[...])
    gs = _sigmoid(gs_ref[...].astype(F32))
    ga = _sigmoid(ga_ref[...].astype(F32))
    o_ref[...] = (gs * a + ga * b).astype(BF16)


def _merge(y_ssm, o_att, w_bs, w_ba, proj):
    t = y_ssm.shape[0]
    tm = _tile(t, 512)
    tn = 1024
    return pl.pallas_call(
        _merge_kernel,
        grid=(t // tm, D_MODEL // tn),
        in_specs=[
            pl.BlockSpec((tm, SSM_D_INNER), lambda i, j: (i, 0)),
            pl.BlockSpec((tm, D_MODEL), lambda i, j: (i, 0)),
            pl.BlockSpec((SSM_D_INNER, tn), lambda i, j: (0, j)),
            pl.BlockSpec((D_MODEL, tn), lambda i, j: (0, j)),
            pl.BlockSpec((tm, tn), lambda i, j: (i, P_GS // tn + j)),
            pl.BlockSpec((tm, tn), lambda i, j: (i, P_GA // tn + j)),
        ],
        out_specs=pl.BlockSpec((tm, tn), lambda i, j: (i, j)),
        out_shape=jax.ShapeDtypeStruct((t, D_MODEL), BF16),
        compiler_params=_params(("parallel", "arbitrary")),
        name="branch_merge",
    )(y_ssm, o_att, w_bs, w_ba, proj, proj)


def _out_proj_kernel(mix_ref, wo_ref, x_ref, nw_ref, x1_ref, h_ref):
    x1 = x_ref[...] + _dot(mix_ref[...], wo_ref[...])
    x1_ref[...] = x1
    ms = jnp.mean(x1 * x1, axis=-1, keepdims=True)
    h_ref[...] = (x1 * lax.rsqrt(ms + EPS) * nw_ref[...]).astype(BF16)


def _out_proj(mixed, w_o, x2, norm_w):
    t = x2.shape[0]
    tm = _tile(t, 512)
    row = lambda i: (i, 0)
    const = lambda i: (0, 0)
    return pl.pallas_call(
        _out_proj_kernel,
        grid=(t // tm,),
        in_specs=[
            pl.BlockSpec((tm, D_MODEL), row),
            pl.BlockSpec((D_MODEL, D_MODEL), const),
            pl.BlockSpec((tm, D_MODEL), row),
            pl.BlockSpec((1, D_MODEL), const),
        ],
        out_specs=[pl.BlockSpec((tm, D_MODEL), row), pl.BlockSpec((tm, D_MODEL), row)],
        out_shape=[jax.ShapeDtypeStruct((t, D_MODEL), F32), jax.ShapeDtypeStruct((t, D_MODEL), BF16)],
        compiler_params=_params(("parallel",)),
        name="out_proj",
    )(mixed, w_o, x2, norm_w)


def _ffn_kernel(h_ref, wg_ref, wu_ref, wd_ref, x_ref, o_ref):
    j = pl.program_id(1)
    h = h_ref[...]
    gate = _dot(h, wg_ref[...])
    up = _dot(h, wu_ref[...])
    part = _dot((gate * _sigmoid(gate) * up).astype(BF16), wd_ref[...])

    @pl.when(j == 0)
    def _():
        o_ref[...] = x_ref[...] + part

    @pl.when(j > 0)
    def _():
        o_ref[...] += part


def _ffn(h2, w_gu, w_d, x1):
    t = h2.shape[0]
    tm = _tile(t, 512)
    th = 512
    nh = FFN_HIDDEN // th
    return pl.pallas_call(
        _ffn_kernel,
        grid=(t // tm, nh),
        in_specs=[
            pl.BlockSpec((tm, D_MODEL), lambda i, j: (i, 0)),
            pl.BlockSpec((D_MODEL, th), lambda i, j: (0, j)),
            pl.BlockSpec((D_MODEL, th), lambda i, j: (0, nh + j)),
            pl.BlockSpec((th, D_MODEL), lambda i, j: (j, 0)),
            pl.BlockSpec((tm, D_MODEL), lambda i, j: (i, 0)),
        ],
        out_specs=pl.BlockSpec((tm, D_MODEL), lambda i, j: (i, 0)),
        out_shape=jax.ShapeDtypeStruct((t, D_MODEL), F32),
        compiler_params=_params(("parallel", "arbitrary")),
        name="ffn",
    )(h2, w_gu, w_gu, w_d, x1)


def _ple_kernel(x_ref, nw_ref, wg_ref, p_ref, wp_ref, o_ref):
    x = x_ref[...]
    ms = jnp.mean(x * x, axis=-1, keepdims=True)
    h = (x * lax.rsqrt(ms + EPS) * nw_ref[...]).astype(BF16)
    gate = _sigmoid(_dot(h, wg_ref[...]))
    o_ref[...] = x + gate * _dot(p_ref[...].astype(BF16), wp_ref[...])


def _ple(x2, norm_w, w_gate, p2, w_ple):
    t = x2.shape[0]
    tm = _tile(t, 512)
    row = lambda i: (i, 0)
    const = lambda i: (0, 0)
    return pl.pallas_call(
        _ple_kernel,
        grid=(t // tm,),
        in_specs=[
            pl.BlockSpec((tm, D_MODEL), row),
            pl.BlockSpec((1, D_MODEL), const),
            pl.BlockSpec((D_MODEL, D_MODEL), const),
            pl.BlockSpec((tm, PLE_DIM), row),
            pl.BlockSpec((PLE_DIM, D_MODEL), const),
        ],
        out_specs=pl.BlockSpec((tm, D_MODEL), row),
        out_shape=jax.ShapeDtypeStruct((t, D_MODEL), F32),
        compiler_params=_params(("parallel",)),
        name="ple",
    )(x2, norm_w, w_gate, p2, w_ple)


def _pad_lanes(v, fill=0.0):
    out = jnp.full((1, LANES), fill, F32)
    return out.at[0, :v.shape[0]].set(v.astype(F32))


def _rope_tables(positions):
    half = ROT_DIM // 2
    inv_freq = ROPE_THETA ** (-jnp.arange(0, ROT_DIM, 2, dtype=F32) / ROT_DIM)
    ang = positions.astype(F32)[..., None] * inv_freq
    cos, sin = jnp.cos(ang), jnp.sin(ang)
    b, l = positions.shape
    rest = ATT_QK_DIM - ROT_DIM
    ones = jnp.ones((b, l, rest), F32)
    zeros = jnp.zeros((b, l, rest), F32)
    z8 = jnp.zeros((b, l, half), F32)
    cos_m = jnp.concatenate([cos, cos, ones], axis=-1)
    s1_m = jnp.concatenate([z8, sin, zeros], axis=-1)
    s2_m = jnp.concatenate([-sin, z8, zeros], axis=-1)
    flat = lambda m: jnp.concatenate([m, m], axis=-1).reshape(b * l, LANES)
    return flat(cos_m), flat(s1_m), flat(s2_m)


def kernel(x, p, positions, w_in, mix_norm_w, conv_w, conv_b, dt_bias, a_log, d_skip, ssm_norm_w, q_norm_w, k_norm_w, lambda_q1, lambda_k1, lambda_q2, lambda_k2, subln_w, w_br_ssm, w_br_attn, w_out, ffn_norm_w, w_gate_up, w_down, ple_norm_w, w_ple_gate, w_ple):
    depth = w_in.shape[0]
    batch, seq, _ = x.shape
    t = batch * seq
    nct = t // SSD_CHUNK
    blk = _tile(seq, ATT_Q_BLOCK)

    cos_t, s1_t, s2_t = _rope_tables(positions)
    head_of_row = jnp.arange(2 * LANES)[:, None] % LANES
    head_of_col = jnp.arange(GROUP_WIDTH)[None, :] // SSM_HEAD_DIM
    expand = jnp.stack([(head_of_row == g * HEADS_PER_GROUP + head_of_col) for g in range(SSM_GROUPS)]
                       ).astype(BF16)

    xr = x.reshape(t, D_MODEL)
    for i in range(depth):
        lambda_init = 0.8 - 0.6 * math.exp(-0.3 * i)
        wi = w_in[i]
        w_main = _in_proj_tiles(wi, 2048)
        w_dt = jnp.pad(wi[:, OFF_XBC:OFF_DT], ((0, 0), (0, LANES - SSM_HEADS))).astype(BF16)

        proj, dt_raw = _in_proj(xr, mix_norm_w[i][None, :], w_main, w_dt)

        a_neg = _pad_lanes(-jnp.exp(a_log[i].astype(F32)) * math.log2(math.e))
        dtc, acsc, dtr, acsr = _dt_prep(dt_raw, _pad_lanes(dt_bias[i]), a_neg)
        grp = lambda m: m[:, :SSM_HEADS, :].reshape(nct, SSM_GROUPS, HEADS_PER_GROUP, SSD_CHUNK)
        cw = conv_w[i].astype(F32)
        cbias = conv_b[i].astype(F32)[None, :]
        y_ssm = _ssd(
            proj, acsc, dtc, grp(acsr), grp(dtr), expand,
            cw[:, :SSM_D_INNER], cw[:, SSM_D_INNER:SSM_D_INNER + SSM_BC], cw[:, SSM_D_INNER + SSM_BC:],
            cbias[:, :SSM_D_INNER], cbias[:, SSM_D_INNER:SSM_D_INNER + SSM_BC], cbias[:, SSM_D_INNER + SSM_BC:],
            jnp.repeat(d_skip[i].astype(F32), SSM_HEAD_DIM)[None, :], ssm_norm_w[i].astype(F32)[None, :],
            batch, seq)

        lam = (jnp.exp(jnp.sum(lambda_q1[i].astype(F32) * lambda_k1[i].astype(F32)))
               - jnp.exp(jnp.sum(lambda_q2[i].astype(F32) * lambda_k2[i].astype(F32))) + lambda_init)
        two = lambda v: jnp.concatenate([v, v]).astype(F32)[None, :]
        sw = jnp.broadcast_to(subln_w[i].astype(F32)[:, None], (ATT_V_DIM, blk))
        o_att = _attention(proj, lam.reshape(1), cos_t, s1_t, s2_t, two(q_norm_w[i]), two(k_norm_w[i]),
                           sw, batch, seq, lambda_init)

        mixed = _merge(y_ssm, o_att, w_br_ssm[i].astype(BF16), w_br_attn[i].astype(BF16), proj)
        x1, h2 = _out_proj(mixed, w_out[i].astype(BF16), xr, ffn_norm_w[i][None, :])
        x2 = _ffn(h2, w_gate_up[i].astype(BF16), w_down[i].astype(BF16), x1)
        xr = _ple(x2, ple_norm_w[i][None, :], w_ple_gate[i].astype(BF16),
                  p[i].reshape(t, PLE_DIM), w_ple[i].astype(BF16))
    return xr.reshape(batch, seq, D_MODEL)
```

```python
import functools
import math

import jax
import jax.numpy as jnp
from jax import lax
from jax.experimental import pallas as pl
from jax.experimental.pallas import tpu as pltpu

F32 = jnp.float32
BF16 = jnp.bfloat16

D_MODEL = 2048
SSM_D_INNER = 4096
SSM_HEAD_DIM = 64
SSM_HEADS = 64
SSM_GROUPS = 8
SSM_STATE = 128
SSM_CONV = 4
SSM_BC = SSM_GROUPS * SSM_STATE
GROUP_WIDTH = SSM_D_INNER // SSM_GROUPS
HEADS_PER_GROUP = SSM_HEADS // SSM_GROUPS
ATT_HEADS = 16
ATT_QK_DIM = 64
ATT_V_DIM = 128
ROPE_THETA = 500000.0
ROT_DIM = 16
FFN_HIDDEN = 5632
PLE_DIM = 256
EPS = 1e-6
SUBLN_EPS = 1e-5

OFF_Z = SSM_D_INNER
OFF_XBC = OFF_Z + SSM_D_INNER + 2 * SSM_BC
OFF_DT = OFF_XBC + SSM_HEADS
P_Z = 0
P_XS = P_Z + SSM_D_INNER
P_B = P_XS + SSM_D_INNER
P_C = P_B + SSM_BC
P_Q = P_C + SSM_BC
P_K = P_Q + ATT_HEADS * 2 * ATT_QK_DIM
P_V = P_K + ATT_HEADS * 2 * ATT_QK_DIM
P_GS = P_V + ATT_HEADS * ATT_V_DIM
P_GA = P_GS + D_MODEL
P_WIDTH = P_GA + D_MODEL

LANES = 128
SSD_CHUNK = 128
SSD_GROUPS_PER_STEP = 8
ATT_Q_BLOCK = 1024
ATT_K_BLOCK = 256
ATT_HEADS_PER_STEP = 2
ATT_SUM_ROWS = 16
VMEM_LIMIT = 56 * 1024 * 1024


def _tile(n, want):
    t = min(n, want)
    while n % t:
        t -= 1
    return t


def _params(sem):
    return pltpu.CompilerParams(dimension_semantics=sem, vmem_limit_bytes=VMEM_LIMIT)


def _sigmoid(v):
    return 1.0 / (1.0 + jnp.exp2(v * (-math.log2(math.e))))


def _dot(a, b):
    return jnp.dot(a, b, preferred_element_type=F32)


def _split_bf16(v):
    hi = v.astype(BF16)
    lo = (v - hi.astype(F32)).astype(BF16)
    return hi, lo


def _in_proj_kernel(x_ref, nw_ref, w_ref, wdt_ref, o_ref, dt_ref, h_scr):
    @pl.when(pl.program_id(1) == 0)
    def _():
        x = x_ref[...]
        ms = jnp.mean(x * x, axis=-1, keepdims=True)
        h = (x * lax.rsqrt(ms + EPS) * nw_ref[...]).astype(BF16)
        h_scr[...] = h
        dt_ref[...] = _dot(h, wdt_ref[...])

    o_ref[...] = _dot(h_scr[...], w_ref[...]).astype(BF16)


def _in_proj_tiles(wi, tn):
    assert OFF_XBC % tn == 0 and P_WIDTH % tn == 0
    starts = [c if c < OFF_XBC else c + SSM_HEADS for c in range(0, P_WIDTH, tn)]
    return jnp.stack([wi[:, s:s + tn] for s in starts]).astype(BF16)


def _in_proj(x2, norm_w, w_tiles, w_dt):
    t = x2.shape[0]
    tm = _tile(t, 1024)
    tn = w_tiles.shape[2]
    return pl.pallas_call(
        _in_proj_kernel,
        grid=(t // tm, P_WIDTH // tn),
        in_specs=[
            pl.BlockSpec((tm, D_MODEL), lambda i, j: (i, 0)),
            pl.BlockSpec((1, D_MODEL), lambda i, j: (0, 0)),
            pl.BlockSpec((None, D_MODEL, tn), lambda i, j: (j, 0, 0)),
            pl.BlockSpec((D_MODEL, LANES), lambda i, j: (0, 0)),
        ],
        out_specs=[
            pl.BlockSpec((tm, tn), lambda i, j: (i, j)),
            pl.BlockSpec((tm, LANES), lambda i, j: (i, 0)),
        ],
        out_shape=[
            jax.ShapeDtypeStruct((t, P_WIDTH), BF16),
            jax.ShapeDtypeStruct((t, LANES), F32),
        ],
        scratch_shapes=[pltpu.VMEM((tm, D_MODEL), BF16)],
        compiler_params=_params(("parallel", "arbitrary")),
        name="in_proj",
    )(x2, norm_w, w_tiles, w_dt)


def _dt_kernel(raw_ref, bias_ref, a_ref, dtc_ref, acsc_ref, dtr_ref, acsr_ref, *, nchunk):
    row = lax.broadcasted_iota(jnp.int32, (SSD_CHUNK, LANES), 0)
    for c in range(nchunk):
        rows = slice(c * SSD_CHUNK, (c + 1) * SSD_CHUNK)
        v = raw_ref[rows, :] + bias_ref[...]
        dt = jnp.maximum(v, 0.0) + jnp.log(1.0 + jnp.exp(-jnp.abs(v)))
        acs = dt * a_ref[...]
        shift = 1
        while shift < SSD_CHUNK:
            acs = acs + jnp.where(row >= shift, pltpu.roll(acs, shift, 0), 0.0)
            shift *= 2
        dtc_ref[rows, :] = dt
        acsc_ref[rows, :] = acs
        dtr_ref[c] = dt.T
        acsr_ref[c] = acs.T


def _dt_prep(dt_raw, dt_bias, a_neg):
    t = dt_raw.shape[0]
    rb = _tile(t, 1024)
    nchunk = rb // SSD_CHUNK
    nct = t // SSD_CHUNK
    col = jax.ShapeDtypeStruct((t, LANES), F32)
    rowm = jax.ShapeDtypeStruct((nct, LANES, SSD_CHUNK), F32)
    return pl.pallas_call(
        functools.partial(_dt_kernel, nchunk=nchunk),
        grid=(t // rb,),
        in_specs=[
            pl.BlockSpec((rb, LANES), lambda i: (i, 0)),
            pl.BlockSpec((1, LANES), lambda i: (0, 0)),
            pl.BlockSpec((1, LANES), lambda i: (0, 0)),
        ],
        out_specs=[
            pl.BlockSpec((rb, LANES), lambda i: (i, 0)),
            pl.BlockSpec((rb, LANES), lambda i: (i, 0)),
            pl.BlockSpec((nchunk, LANES, SSD_CHUNK), lambda i: (i, 0, 0)),
            pl.BlockSpec((nchunk, LANES, SSD_CHUNK), lambda i: (i, 0, 0)),
        ],
        out_shape=[col, col, rowm, rowm],
        compiler_params=_params(("parallel",)),
        name="dt_prep",
    )(dt_raw, dt_bias, a_neg)


def _ssd_kernel(z_ref, xs_ref, b_ref, c_ref, acsc_ref, dtc_ref, acsr_ref, dtr_ref, e_ref,
                cwx_ref, cwb_ref, cwc_ref, cbx_ref, cbb_ref, cbc_ref, dsk_ref, nw_ref,
                y_ref, state, winx, winb, winc, *, gps):
    cs = SSD_CHUNK
    gw = GROUP_WIDTH
    n = SSM_STATE
    chunk = pl.program_id(1)
    g0 = pl.program_id(2) * gps

    @pl.when(chunk == 0)
    def _():
        for gi in range(gps):
            state[g0 + gi] = jnp.zeros(state.shape[1:], F32)
            winx[g0 + gi, 0:8, :] = jnp.zeros((8, gw), F32)
            winb[g0 + gi, 0:8, :] = jnp.zeros((8, n), F32)
            winc[g0 + gi, 0:8, :] = jnp.zeros((8, n), F32)

    acsc = acsc_ref[...]
    last = acsc[cs - 1:cs, :]
    e1 = jnp.exp2(acsc)
    wte = dtc_ref[...] * jnp.exp2(last - acsc)
    dec = jnp.broadcast_to(jnp.exp2(last), (16, LANES))
    hi, lo = _split_bf16(jnp.concatenate([e1, wte, dec], axis=0))
    factors = jnp.concatenate([hi, lo], axis=1)

    ri = lax.broadcasted_iota(jnp.int32, (cs, cs), 0)
    ci = lax.broadcasted_iota(jnp.int32, (cs, cs), 1)
    causal = ri >= ci
    lane = lax.broadcasted_iota(jnp.int32, (cs, LANES), 1)

    def conv_silu(u, win, g, w, bias):
        win[g, 8:8 + cs, :] = u
        acc = u * w[SSM_CONV - 1:SSM_CONV, :] + bias
        for s in range(1, SSM_CONV):
            acc = acc + win[g, 8 - s:8 - s + cs, :] * w[SSM_CONV - 1 - s:SSM_CONV - s, :]
        win[g, 0:8, :] = u[cs - 8:cs, :]
        return acc * _sigmoid(acc)

    for gi in range(gps):
        g = g0 + gi
        wide = slice(gi * gw, (gi + 1) * gw)
        narrow = slice(gi * n, (gi + 1) * n)
        xc = conv_silu(xs_ref[:, wide].astype(F32), winx, g, cwx_ref[:, wide], cbx_ref[:, wide])
        bc = conv_silu(b_ref[:, narrow].astype(F32), winb, g, cwb_ref[:, narrow], cbb_ref[:, narrow])
        cc = conv_silu(c_ref[:, narrow].astype(F32), winc, g, cwc_ref[:, narrow], cbc_ref[:, narrow])
        bt_b = bc.T.astype(BF16)
        cc_b = cc.astype(BF16)
        cb = _dot(cc_b, bt_b)

        ex = _dot(factors, e_ref[gi])
        e1_e = ex[0:cs, :]
        wte_e = ex[cs:2 * cs, :]
        dec_e = ex[2 * cs:2 * cs + 1, :]

        acsr = acsr_ref[gi]
        dtr = dtr_ref[gi]
        pieces = []
        for m in range(HEADS_PER_GROUP // 2):
            mats = []
            for k in (2 * m, 2 * m + 1):
                acs_j = jnp.broadcast_to(acsr[k:k + 1, :], (cs, cs))
                seg = acs_j.T - acs_j
                decay = jnp.exp2(jnp.where(causal, seg, -jnp.inf))
                dt_j = jnp.broadcast_to(dtr[k:k + 1, :], (cs, cs))
                mats.append((cb * decay * dt_j).astype(BF16))
            xp = xc[:, m * LANES:(m + 1) * LANES]
            x_lo = jnp.where(lane < SSM_HEAD_DIM, xp, 0.0).astype(BF16)
            x_hi = jnp.where(lane >= SSM_HEAD_DIM, xp, 0.0).astype(BF16)
            pieces.append(_dot(jnp.concatenate(mats, axis=1), jnp.concatenate([x_lo, x_hi], axis=0)))
        y_diag = jnp.concatenate(pieces, axis=1)

        st = state[g]
        y_off = _dot(cc_b, st.astype(BF16)) * e1_e
        y = y_diag + y_off + dsk_ref[:, wide] * xc
        z = z_ref[:, wide].astype(F32)
        y = y * (z * _sigmoid(z))
        ms = jnp.mean(y * y, axis=-1, keepdims=True)
        y_ref[:, wide] = (y * lax.rsqrt(ms + EPS) * nw_ref[:, wide]).astype(BF16)

        xw = (xc * wte_e).astype(BF16)
        state[g] = st * dec_e + _dot(bt_b, xw)


def _ssd(proj, acsc, dtc, acsr, dtr, expand, cw_x, cw_b, cw_c, cb_x, cb_b, cb_c, dskip, norm_w,
         batch, seq):
    t = proj.shape[0]
    cs = SSD_CHUNK
    nc = seq // cs
    gps = SSD_GROUPS_PER_STEP
    gw = gps * GROUP_WIDTH
    n = gps * SSM_STATE

    def rowblk(b, c, g):
        return b * nc + c

    in_specs = [
        pl.BlockSpec((cs, gw), lambda b, c, g: (rowblk(b, c, g), P_Z // gw + g)),
        pl.BlockSpec((cs, gw), lambda b, c, g: (rowblk(b, c, g), P_XS // gw + g)),
        pl.BlockSpec((cs, n), lambda b, c, g: (rowblk(b, c, g), P_B // n + g)),
        pl.BlockSpec((cs, n), lambda b, c, g: (rowblk(b, c, g), P_C // n + g)),
        pl.BlockSpec((cs, LANES), lambda b, c, g: (rowblk(b, c, g), 0)),
        pl.BlockSpec((cs, LANES), lambda b, c, g: (rowblk(b, c, g), 0)),
        pl.BlockSpec((None, gps, HEADS_PER_GROUP, cs), lambda b, c, g: (rowblk(b, c, g), g, 0, 0)),
        pl.BlockSpec((None, gps, HEADS_PER_GROUP, cs), lambda b, c, g: (rowblk(b, c, g), g, 0, 0)),
        pl.BlockSpec((gps, 2 * LANES, GROUP_WIDTH), lambda b, c, g: (g, 0, 0)),
        pl.BlockSpec((SSM_CONV, gw), lambda b, c, g: (0, g)),
        pl.BlockSpec((SSM_CONV, n), lambda b, c, g: (0, g)),
        pl.BlockSpec((SSM_CONV, n), lambda b, c, g: (0, g)),
        pl.BlockSpec((1, gw), lambda b, c, g: (0, g)),
        pl.BlockSpec((1, n), lambda b, c, g: (0, g)),
        pl.BlockSpec((1, n), lambda b, c, g: (0, g)),
        pl.BlockSpec((1, gw), lambda b, c, g: (0, g)),
        pl.BlockSpec((1, gw), lambda b, c, g: (0, g)),
    ]
    return pl.pallas_call(
        functools.partial(_ssd_kernel, gps=gps),
        grid=(batch, nc, SSM_GROUPS // gps),
        in_specs=in_specs,
        out_specs=pl.BlockSpec((cs, gw), lambda b, c, g: (rowblk(b, c, g), g)),
        out_shape=jax.ShapeDtypeStruct((t, SSM_D_INNER), BF16),
        scratch_shapes=[
            pltpu.VMEM((SSM_GROUPS, SSM_STATE, GROUP_WIDTH), F32),
            pltpu.VMEM((SSM_GROUPS, 8 + cs, GROUP_WIDTH), F32),
            pltpu.VMEM((SSM_GROUPS, 8 + cs, SSM_STATE), F32),
            pltpu.VMEM((SSM_GROUPS, 8 + cs, SSM_STATE), F32),
        ],
        compiler_params=_params(("parallel", "arbitrary", "arbitrary")),
        name="ssd_scan",
    )(proj, proj, proj, proj, acsc, dtc, acsr, dtr, expand,
      cw_x, cw_b, cw_c, cb_x, cb_b, cb_c, dskip, norm_w)


def _attn_kernel(lam_ref, q_ref, k_ref, v_ref, cos_ref, s1_ref, s2_ref, qw_ref, kw_ref, sw_ref,
                 o_ref, kst, qp, vt, s_a, s_b, acc, m_scr, l_scr, *, bq, bk, nq, lambda_init):
    nkv = nq * (bq // bk)
    lane = lax.broadcasted_iota(jnp.int32, (bk, LANES), 1)
    ra = lax.broadcasted_iota(jnp.int32, (LANES, LANES), 0) // ATT_QK_DIM
    rb = lax.broadcasted_iota(jnp.int32, (LANES, LANES), 1) // ATT_QK_DIM
    half_mean = jnp.where(ra == rb, 1.0 / ATT_QK_DIM, 0.0).astype(BF16)
    q_scale = (ATT_QK_DIM ** -0.5) * math.log2(math.e)

    def norm_rope(t, w, rows):
        ms = _dot((t * t).astype(BF16), half_mean)
        tn = t * lax.rsqrt(ms + EPS) * w
        return (tn * cos_ref[rows, :] + pltpu.roll(tn, ROT_DIM // 2, 1) * s1_ref[rows, :]
                + pltpu.roll(tn, LANES - ROT_DIM // 2, 1) * s2_ref[rows, :])

    ones_row = jnp.where(lax.broadcasted_iota(jnp.int32, (ATT_SUM_ROWS, bk), 0) == 0, 1.0, 0.0).astype(BF16)

    def prep(cols, r2, carry):
        for half in range(bq // bk):
            r = r2 * (bq // bk) + half
            rows = pl.ds(pl.multiple_of(r * bk, bk), bk)
            kb = norm_rope(k_ref[rows, cols].astype(F32), kw_ref[...], rows)
            kst[r, 0:bk, :] = jnp.where(lane < ATT_QK_DIM, kb, 0.0).astype(BF16)
            kst[r, bk:2 * bk, :] = jnp.where(lane >= ATT_QK_DIM, kb, 0.0).astype(BF16)
            qb = norm_rope(q_ref[rows, cols].astype(F32), qw_ref[...], rows)
            qp[rows, :] = (qb * q_scale).astype(BF16)
            vt[r, 0:ATT_V_DIM, :] = v_ref[rows, cols].T
            vt[r, ATT_V_DIM:, :] = ones_row
        return carry

    nsub = bq // bk
    bufs = (s_a, s_b)
    kidx = lax.broadcasted_iota(jnp.int32, (bk, bk), 0)
    qidx = lax.broadcasted_iota(jnp.int32, (bk, bk), 1)
    tri_mask = kidx <= qidx
    lam = lam_ref[0]

    def q_rows(qi):
        return pl.ds(pl.multiple_of(qi * bq, bq), bq)

    def produce(dst, qblk, kj, lo):
        dst[:, lo:] = lax.dot_general(kst[kj], qblk[lo:, :], (((1,), (1,)), ((), ())),
                                      preferred_element_type=F32)

    def consume(src, kj, lo, diagonal):
        def logits(c):
            sc = src[c * bk:(c + 1) * bk, lo:]
            if diagonal:
                head = jnp.where(tri_mask, sc[:, :bk], -jnp.inf)
                sc = head if lo + bk == bq else jnp.concatenate([head, sc[:, bk:]], axis=1)
            return sc

        for c in range(2):
            m_old = m_scr[c, :, lo:]
            m_new = jnp.maximum(m_old, jnp.max(logits(c), axis=0, keepdims=True))
            alpha = jnp.exp2(m_old - m_new)
            p = jnp.exp2(logits(c) - m_new)
            pv = _dot(vt[kj], p.astype(BF16))
            l_scr[c, :, lo:] = alpha * l_scr[c, :, lo:] + pv[ATT_V_DIM:ATT_V_DIM + 1, :]
            acc[c, :, lo:] = alpha * acc[c, :, lo:] + pv[0:ATT_V_DIM, :]
            m_scr[c, :, lo:] = m_new

    def q_body(cols, qi, carry):
        qblk = qp[q_rows(qi), :]
        m_scr[...] = jnp.full(m_scr.shape, -jnp.inf, F32)
        l_scr[...] = jnp.zeros(l_scr.shape, F32)
        acc[...] = jnp.zeros(acc.shape, F32)

        def trip(t, c2):
            for j in range(nsub):
                kj = t * nsub + j
                produce(bufs[(j + 1) % 2], qblk, kj + 1, 0)
                consume(bufs[j % 2], kj, 0, False)
            return c2

        lax.fori_loop(0, qi, trip, 0)
        for j in range(nsub):
            kj = qi * nsub + j
            if j + 1 < nsub:
                produce(bufs[(j + 1) % 2], qblk, kj + 1, (j + 1) * bk)
            else:
                produce(bufs[0], qp[q_rows(jnp.minimum(qi + 1, nq - 1)), :], 0, 0)
            consume(bufs[j % 2], kj, j * bk, True)

        o = acc[0] * (1.0 / l_scr[0]) - acc[1] * (lam / l_scr[1])
        ms = jnp.mean(o * o, axis=0, keepdims=True)
        on = o * lax.rsqrt(ms + SUBLN_EPS) * sw_ref[...] * (1.0 - lambda_init)
        o_ref[q_rows(qi), cols] = on.T.astype(BF16)
        return carry

    for head in range(q_ref.shape[1] // LANES):
        cols = slice(head * LANES, (head + 1) * LANES)
        lax.fori_loop(0, nq, functools.partial(prep, cols), 0)
        produce(s_a, qp[q_rows(0), :], 0, 0)
        lax.fori_loop(0, nq, functools.partial(q_body, cols), 0)


def _attention(proj, lam, cos_t, s1_t, s2_t, qw, kw, sw, batch, seq, lambda_init):
    t = proj.shape[0]
    bq = _tile(seq, ATT_Q_BLOCK)
    bk = min(ATT_K_BLOCK, bq // 2)
    nq = seq // bq
    nkv = seq // bk
    width = ATT_HEADS_PER_STEP * LANES
    qoff, koff, voff = P_Q // width, P_K // width, P_V // width
    head_spec = lambda off: pl.BlockSpec((seq, width), lambda b, h: (b, off + h))
    tab_spec = pl.BlockSpec((seq, LANES), lambda b, h: (b, 0))
    return pl.pallas_call(
        functools.partial(_attn_kernel, bq=bq, bk=bk, nq=nq, lambda_init=lambda_init),
        grid=(batch, ATT_HEADS // ATT_HEADS_PER_STEP),
        in_specs=[
            pl.BlockSpec(memory_space=pltpu.SMEM),
            head_spec(qoff), head_spec(koff), head_spec(voff),
            tab_spec, tab_spec, tab_spec,
            pl.BlockSpec((1, LANES), lambda b, h: (0, 0)),
            pl.BlockSpec((1, LANES), lambda b, h: (0, 0)),
            pl.BlockSpec((ATT_V_DIM, bq), lambda b, h: (0, 0)),
        ],
        out_specs=pl.BlockSpec((seq, width), lambda b, h: (b, h)),
        out_shape=jax.ShapeDtypeStruct((t, ATT_HEADS * ATT_V_DIM), BF16),
        scratch_shapes=[
            pltpu.VMEM((nkv, 2 * bk, LANES), BF16),
            pltpu.VMEM((seq, LANES), BF16),
            pltpu.VMEM((nkv, ATT_V_DIM + ATT_SUM_ROWS, bk), BF16),
            pltpu.VMEM((2 * bk, bq), F32),
            pltpu.VMEM((2 * bk, bq), F32),
            pltpu.VMEM((2, ATT_V_DIM, bq), F32),
            pltpu.VMEM((2, 1, bq), F32),
            pltpu.VMEM((2, 1, bq), F32),
        ],
        compiler_params=_params(("parallel", "arbitrary")),
        name="diff_attention",
    )(lam, proj, proj, proj, cos_t, s1_t, s2_t, qw, kw, sw)


def _merge_kernel(ys_ref, oa_ref, wbs_ref, wba_ref, gs_ref, ga_ref, o_ref):
    a = _dot(ys_ref[...], wbs_ref[...])
    b = _dot(oa_ref[...], wba_ref[...])
    gs = _sigmoid(gs_ref[...].astype(F32))
    ga = _sigmoid(ga_ref[...].astype(F32))
    o_ref[...] = (gs * a + ga * b).astype(BF16)


def _merge(y_ssm, o_att, w_bs, w_ba, proj):
    t = y_ssm.shape[0]
    tm = _tile(t, 512)
    tn = 1024
    return pl.pallas_call(
        _merge_kernel,
        grid=(t // tm, D_MODEL // tn),
        in_specs=[
            pl.BlockSpec((tm, SSM_D_INNER), lambda i, j: (i, 0)),
            pl.BlockSpec((tm, D_MODEL), lambda i, j: (i, 0)),
            pl.BlockSpec((SSM_D_INNER, tn), lambda i, j: (0, j)),
            pl.BlockSpec((D_MODEL, tn), lambda i, j: (0, j)),
            pl.BlockSpec((tm, tn), lambda i, j: (i, P_GS // tn + j)),
            pl.BlockSpec((tm, tn), lambda i, j: (i, P_GA // tn + j)),
        ],
        out_specs=pl.BlockSpec((tm, tn), lambda i, j: (i, j)),
        out_shape=jax.ShapeDtypeStruct((t, D_MODEL), BF16),
        compiler_params=_params(("parallel", "arbitrary")),
        name="branch_merge",
    )(y_ssm, o_att, w_bs, w_ba, proj, proj)


def _out_proj_kernel(mix_ref, wo_ref, x_ref, nw_ref, x1_ref, h_ref):
    x1 = x_ref[...] + _dot(mix_ref[...], wo_ref[...])
    x1_ref[...] = x1
    ms = jnp.mean(x1 * x1, axis=-1, keepdims=True)
    h_ref[...] = (x1 * lax.rsqrt(ms + EPS) * nw_ref[...]).astype(BF16)


def _out_proj(mixed, w_o, x2, norm_w):
    t = x2.shape[0]
    tm = _tile(t, 512)
    row = lambda i: (i, 0)
    const = lambda i: (0, 0)
    return pl.pallas_call(
        _out_proj_kernel,
        grid=(t // tm,),
        in_specs=[
            pl.BlockSpec((tm, D_MODEL), row),
            pl.BlockSpec((D_MODEL, D_MODEL), const),
            pl.BlockSpec((tm, D_MODEL), row),
            pl.BlockSpec((1, D_MODEL), const),
        ],
        out_specs=[pl.BlockSpec((tm, D_MODEL), row), pl.BlockSpec((tm, D_MODEL), row)],
        out_shape=[jax.ShapeDtypeStruct((t, D_MODEL), F32), jax.ShapeDtypeStruct((t, D_MODEL), BF16)],
        compiler_params=_params(("parallel",)),
        name="out_proj",
    )(mixed, w_o, x2, norm_w)


def _ffn_kernel(h_ref, wg_ref, wu_ref, wd_ref, x_ref, o_ref):
    j = pl.program_id(1)
    h = h_ref[...]
    gate = _dot(h, wg_ref[...])
    up = _dot(h, wu_ref[...])
    part = _dot((gate * _sigmoid(gate) * up).astype(BF16), wd_ref[...])

    @pl.when(j == 0)
    def _():
        o_ref[...] = x_ref[...] + part

    @pl.when(j > 0)
    def _():
        o_ref[...] += part


def _ffn(h2, w_gu, w_d, x1):
    t = h2.shape[0]
    tm = _tile(t, 512)
    th = 512
    nh = FFN_HIDDEN // th
    return pl.pallas_call(
        _ffn_kernel,
        grid=(t // tm, nh),
        in_specs=[
            pl.BlockSpec((tm, D_MODEL), lambda i, j: (i, 0)),
            pl.BlockSpec((D_MODEL, th), lambda i, j: (0, j)),
            pl.BlockSpec((D_MODEL, th), lambda i, j: (0, nh + j)),
            pl.BlockSpec((th, D_MODEL), lambda i, j: (j, 0)),
            pl.BlockSpec((tm, D_MODEL), lambda i, j: (i, 0)),
        ],
        out_specs=pl.BlockSpec((tm, D_MODEL), lambda i, j: (i, 0)),
        out_shape=jax.ShapeDtypeStruct((t, D_MODEL), F32),
        compiler_params=_params(("parallel", "arbitrary")),
        name="ffn",
    )(h2, w_gu, w_gu, w_d, x1)


def _ple_kernel(x_ref, nw_ref, wg_ref, p_ref, wp_ref, o_ref):
    x = x_ref[...]
    ms = jnp.mean(x * x, axis=-1, keepdims=True)
    h = (x * lax.rsqrt(ms + EPS) * nw_ref[...]).astype(BF16)
    gate = _sigmoid(_dot(h, wg_ref[...]))
    o_ref[...] = x + gate * _dot(p_ref[...].astype(BF16), wp_ref[...])


def _ple(x2, norm_w, w_gate, p2, w_ple):
    t = x2.shape[0]
    tm = _tile(t, 512)
    row = lambda i: (i, 0)
    const = lambda i: (0, 0)
    return pl.pallas_call(
        _ple_kernel,
        grid=(t // tm,),
        in_specs=[
            pl.BlockSpec((tm, D_MODEL), row),
            pl.BlockSpec((1, D_MODEL), const),
            pl.BlockSpec((D_MODEL, D_MODEL), const),
            pl.BlockSpec((tm, PLE_DIM), row),
            pl.BlockSpec((PLE_DIM, D_MODEL), const),
        ],
        out_specs=pl.BlockSpec((tm, D_MODEL), row),
        out_shape=jax.ShapeDtypeStruct((t, D_MODEL), F32),
        compiler_params=_params(("parallel",)),
        name="ple",
    )(x2, norm_w, w_gate, p2, w_ple)


def _pad_lanes(v, fill=0.0):
    out = jnp.full((1, LANES), fill, F32)
    return out.at[0, :v.shape[0]].set(v.astype(F32))


def _rope_tables(positions):
    half = ROT_DIM // 2
    inv_freq = ROPE_THETA ** (-jnp.arange(0, ROT_DIM, 2, dtype=F32) / ROT_DIM)
    ang = positions.astype(F32)[..., None] * inv_freq
    cos, sin = jnp.cos(ang), jnp.sin(ang)
    b, l = positions.shape
    rest = ATT_QK_DIM - ROT_DIM
    ones = jnp.ones((b, l, rest), F32)
    zeros = jnp.zeros((b, l, rest), F32)
    z8 = jnp.zeros((b, l, half), F32)
    cos_m = jnp.concatenate([cos, cos, ones], axis=-1)
    s1_m = jnp.concatenate([z8, sin, zeros], axis=-1)
    s2_m = jnp.concatenate([-sin, z8, zeros], axis=-1)
    flat = lambda m: jnp.concatenate([m, m], axis=-1).reshape(b * l, LANES)
    return flat(cos_m), flat(s1_m), flat(s2_m)


def kernel(x, p, positions, w_in, mix_norm_w, conv_w, conv_b, dt_bias, a_log, d_skip, ssm_norm_w, q_norm_w, k_norm_w, lambda_q1, lambda_k1, lambda_q2, lambda_k2, subln_w, w_br_ssm, w_br_attn, w_out, ffn_norm_w, w_gate_up, w_down, ple_norm_w, w_ple_gate, w_ple):
    depth = w_in.shape[0]
    batch, seq, _ = x.shape
    t = batch * seq
    nct = t // SSD_CHUNK
    blk = _tile(seq, ATT_Q_BLOCK)

    cos_t, s1_t, s2_t = _rope_tables(positions)
    head_of_row = jnp.arange(2 * LANES)[:, None] % LANES
    head_of_col = jnp.arange(GROUP_WIDTH)[None, :] // SSM_HEAD_DIM
    expand = jnp.stack([(head_of_row == g * HEADS_PER_GROUP + head_of_col) for g in range(SSM_GROUPS)]
                       ).astype(BF16)

    xr = x.reshape(t, D_MODEL)
    for i in range(depth):
        lambda_init = 0.8 - 0.6 * math.exp(-0.3 * i)
        wi = w_in[i]
        w_main = _in_proj_tiles(wi, 2048)
        w_dt = jnp.pad(wi[:, OFF_XBC:OFF_DT], ((0, 0), (0, LANES - SSM_HEADS))).astype(BF16)

        proj, dt_raw = _in_proj(xr, mix_norm_w[i][None, :], w_main, w_dt)

        a_neg = _pad_lanes(-jnp.exp(a_log[i].astype(F32)) * math.log2(math.e))
        dtc, acsc, dtr, acsr = _dt_prep(dt_raw, _pad_lanes(dt_bias[i]), a_neg)
        grp = lambda m: m[:, :SSM_HEADS, :].reshape(nct, SSM_GROUPS, HEADS_PER_GROUP, SSD_CHUNK)
        cw = conv_w[i].astype(F32)
        cbias = conv_b[i].astype(F32)[None, :]
        y_ssm = _ssd(
            proj, acsc, dtc, grp(acsr), grp(dtr), expand,
            cw[:, :SSM_D_INNER], cw[:, SSM_D_INNER:SSM_D_INNER + SSM_BC], cw[:, SSM_D_INNER + SSM_BC:],
            cbias[:, :SSM_D_INNER], cbias[:, SSM_D_INNER:SSM_D_INNER + SSM_BC], cbias[:, SSM_D_INNER + SSM_BC:],
            jnp.repeat(d_skip[i].astype(F32), SSM_HEAD_DIM)[None, :], ssm_norm_w[i].astype(F32)[None, :],
            batch, seq)

        lam = (jnp.exp(jnp.sum(lambda_q1[i].astype(F32) * lambda_k1[i].astype(F32)))
               - jnp.exp(jnp.sum(lambda_q2[i].astype(F32) * lambda_k2[i].astype(F32))) + lambda_init)
        two = lambda v: jnp.concatenate([v, v]).astype(F32)[None, :]
        sw = jnp.broadcast_to(subln_w[i].astype(F32)[:, None], (ATT_V_DIM, blk))
        o_att = _attention(proj, lam.reshape(1), cos_t, s1_t, s2_t, two(q_norm_w[i]), two(k_norm_w[i]),
                           sw, batch, seq, lambda_init)

        mixed = _merge(y_ssm, o_att, w_br_ssm[i].astype(BF16), w_br_attn[i].astype(BF16), proj)
        x1, h2 = _out_proj(mixed, w_out[i].astype(BF16), xr, ffn_norm_w[i][None, :])
        x2 = _ffn(h2, w_gate_up[i].astype(BF16), w_down[i].astype(BF16), x1)
        xr = _ple(x2, ple_norm_w[i][None, :], w_ple_gate[i].astype(BF16),
                  p[i].reshape(t, PLE_DIM), w_ple[i].astype(BF16))
    return xr.reshape(batch, seq, D_MODEL)
```
